```python
import jax, jax.numpy as jnp
from jax import lax
import numpy as np

D_MODEL = 1024
BATCH = 16
SEQ = 2048
DEPTH = 2

N_BRANCH = 4
BW = 256
EPS = 1e-6
RWKV_HEADS = 4
RWKV_HEAD = BW // RWKV_HEADS
RWKV_DECAY_LORA = 64
RWKV_A_LORA = 64
RWKV_SHIFT_WIDTH = 3 * BW + RWKV_DECAY_LORA + RWKV_A_LORA
RWKV_LN_EPS = 64e-5
HGRN_HEADS = 4
HGRN_EXPAND = BW // HGRN_HEADS
HGRN_VHEAD = BW // HGRN_HEADS
HGRN_CHUNK = 64
RET_HEADS = 4
RET_VHEAD = BW // RET_HEADS
RET_QKHEAD = RET_VHEAD // 2
RET_QK = RET_HEADS * RET_QKHEAD
RET_CHUNK = 128
ROPE_BASE = 10000.0
LRU_BLOCKS = 4
LRU_BLOCK = BW // LRU_BLOCKS
CONV_WIDTH = 4
LRU_C = 8.0

IN_SPLITS = (
    BW, BW, BW, RWKV_DECAY_LORA, RWKV_A_LORA, BW,
    BW, BW, BW, BW,
    RET_QK, RET_QK, BW, BW,
    BW, BW,
    N_BRANCH * D_MODEL,
)
IN_WIDTH = sum(IN_SPLITS)

kernel_name = "hybrid_rwkv7_hgrn2_retnet_rglru_trunk"


def _rmsnorm(x, g):
    xf = x.astype(jnp.float32)
    y = xf * lax.rsqrt(jnp.mean(xf * xf, axis=-1, keepdims=True) + EPS)
    return y * g.astype(jnp.float32)


def _head_rms(o):
    return o * lax.rsqrt(jnp.mean(o * o, axis=-1, keepdims=True) + EPS)


def _token_shift(u):
    return jnp.pad(u, ((0, 0), (1, 0), (0, 0)))[:, :-1]


def _rwkv7(r, k, v, w_lo, a_lo, mu, w0, w2, a0, a2, k_k, k_a, r_k, ln_g, ln_b):
    B, T, _ = r.shape
    H, N = RWKV_HEADS, RWKV_HEAD
    feats = jnp.concatenate([r, k, v, w_lo, a_lo], -1).astype(jnp.float32)
    feats = feats + mu * (_token_shift(feats) - feats)
    r, k, v, w_lo, a_lo = jnp.split(feats, [BW, 2 * BW, 3 * BW, 3 * BW + RWKV_DECAY_LORA], -1)
    w = -jax.nn.softplus(-(w0 + jnp.tanh(w_lo) @ w2)) - 0.5
    decay = jnp.exp(-jnp.exp(w))
    a = jax.nn.sigmoid(a0 + a_lo @ a2)
    kk = k * k_k
    k = k * (1.0 + (a - 1.0) * k_a)
    hd = lambda t: t.reshape(B, T, H, N)
    r, k, v, kk, a, decay = hd(r), hd(k), hd(v), hd(kk), hd(a), hd(decay)
    kk = kk / jnp.maximum(jnp.sqrt(jnp.sum(kk * kk, -1, keepdims=True)), 1e-12)

    def step(S, inp):
        r_t, w_t, k_t, v_t, kk_t, a_t = inp
        sa = jnp.einsum('bhvk,bhk->bhv', S, -kk_t)
        S = (S * w_t[:, :, None, :] + sa[..., None] * (kk_t * a_t)[:, :, None, :]
             + v_t[..., None] * k_t[:, :, None, :])
        return S, jnp.einsum('bhvk,bhk->bhv', S, r_t)

    tm = lambda t: jnp.moveaxis(t, 1, 0)
    S0 = jnp.zeros((B, H, N, N), jnp.float32)
    _, y = lax.scan(step, S0, (tm(r), tm(decay), tm(k), tm(v), tm(kk), tm(a)))
    y = jnp.moveaxis(y, 0, 1)
    mean = jnp.mean(y, -1, keepdims=True)
    var = jnp.mean(jnp.square(y - mean), -1, keepdims=True)
    y = ((y - mean) * lax.rsqrt(var + RWKV_LN_EPS)).reshape(B, T, BW) * ln_g + ln_b
    bonus = jnp.sum(r * k * r_k.reshape(H, N), -1, keepdims=True) * v
    return y + bonus.reshape(B, T, BW)


def _hgrn2(q, f_pre, i, lb, norm_g):
    B, T, _ = q.shape
    H, K, V, C = HGRN_HEADS, HGRN_EXPAND, HGRN_VHEAD, HGRN_CHUNK
    n = T // C
    f_pre = f_pre.astype(jnp.float32)
    lb = lb.astype(jnp.float32)
    k = (1.0 - lb) * jax.nn.sigmoid(-f_pre)
    log_f = jnp.log1p(-k)
    chunks = lambda t, d: t.astype(jnp.float32).reshape(B, n, C, H, d).transpose(1, 0, 3, 2, 4)
    causal = jnp.tril(jnp.ones((C, C), bool))[:, :, None]

    def step(S, inp):
        q_c, k_c, v_c, lf_c = inp
        b = jnp.cumsum(lf_c, axis=2)
        rel = jnp.where(causal, b[:, :, :, None, :] - b[:, :, None, :, :], -jnp.inf)
        scores = jnp.sum(q_c[:, :, :, None, :] * k_c[:, :, None, :, :] * jnp.exp(rel), -1)
        o = (jnp.einsum('bhts,bhsv->bhtv', scores, v_c)
             + jnp.einsum('bhtk,bhkv->bhtv', q_c * jnp.exp(b), S))
        b_end = b[:, :, -1:, :]
        S = (jnp.exp(b_end)[:, :, 0, :, None] * S
             + jnp.einsum('bhsk,bhsv->bhkv', k_c * jnp.exp(b_end - b), v_c))
        return S, o

    S0 = jnp.zeros((B, H, K, V), jnp.float32)
    _, o = lax.scan(step, S0, (chunks(q, K), chunks(k, K), chunks(i, V), chunks(log_f, K)))
    o = o.transpose(1, 0, 3, 2, 4).reshape(B, T, H, V)
    return _head_rms(o).reshape(B, T, BW) * norm_g


def _rope(t, cos, sin):
    t1, t2 = jnp.split(t, 2, -1)
    return jnp.concatenate([t1 * cos - t2 * sin, t1 * sin + t2 * cos], -1)


def _retention(q, k, v, norm_g):
    B, T, _ = q.shape
    H, Dk, Dv, C = RET_HEADS, RET_QKHEAD, RET_VHEAD, RET_CHUNK
    n = T // C
    pos = jnp.arange(T, dtype=jnp.float32)
    inv_freq = 1.0 / (ROPE_BASE ** jnp.linspace(0.0, 1.0, Dk // 2, dtype=jnp.float32))
    ang = pos[:, None] * inv_freq[None, :]
    cos, sin = jnp.cos(ang)[None, :, None, :], jnp.sin(ang)[None, :, None, :]
    q = _rope(q.astype(jnp.float32).reshape(B, T, H, Dk), cos, sin)
    k = _rope(k.astype(jnp.float32).reshape(B, T, H, Dk), cos, sin) * (Dk ** -0.5)
    v = v.astype(jnp.float32).reshape(B, T, H, Dv)
    chunk = lambda t: t.reshape(B, n, C, H, t.shape[-1]).transpose(0, 3, 1, 2, 4)
    q, k, v = chunk(q), chunk(k), chunk(v)
    log_gamma = jnp.log1p(-jnp.exp2(-5.0 - jnp.arange(H, dtype=jnp.float32)))
    idx = jnp.arange(C, dtype=jnp.float32)
    rel = idx[:, None] - idx[None, :]
    decay = jnp.where(rel >= 0, jnp.exp(log_gamma[:, None, None] * jnp.maximum(rel, 0.0)), 0.0)
    scores = jnp.einsum('bhntd,bhnsd->bhnts', q, k) * decay[:, None]
    o = jnp.einsum('bhnts,bhnse->bhnte', scores, v)
    zeta = jnp.exp(log_gamma[:, None] * (C - 1 - idx))
    xi = jnp.exp(log_gamma[:, None] * (idx + 1.0))
    kv = jnp.einsum('bhnsd,hs,bhnse->bhnde', k, zeta, v)
    gamma_c = jnp.exp(log_gamma * C)[None, :, None, None]

    def step(R, kv_n):
        return gamma_c * R + kv_n, R

    _, R_prev = lax.scan(step, jnp.zeros((B, H, Dk, Dv), jnp.float32), jnp.moveaxis(kv, 2, 0))
    R_prev = jnp.moveaxis(R_prev, 0, 2)
    o = o + jnp.einsum('bhntd,bhnde->bhnte', q * xi[:, None, :, None], R_prev)
    o = o.transpose(0, 2, 3, 1, 4).reshape(B, T, H, Dv)
    return _head_rms(o).reshape(B, T, BW) * norm_g


def _rglru(x, conv_w, conv_b, w_a, b_a, w_x, b_x, lam):
    B, T, _ = x.shape
    x = x.astype(jnp.float32)
    xp = jnp.pad(x, ((0, 0), (CONV_WIDTH - 1, 0), (0, 0)))
    y = conv_b + sum(conv_w[j] * xp[:, j:j + T] for j in range(CONV_WIDTH))
    yb = y.reshape(B, T, LRU_BLOCKS, LRU_BLOCK)
    r = jax.nn.sigmoid(jnp.einsum('btgi,gij->btgj', yb, w_a).reshape(B, T, BW) + b_a)
    i = jax.nn.sigmoid(jnp.einsum('btgi,gij->btgj', yb, w_x).reshape(B, T, BW) + b_x)
    log_a = -LRU_C * r * jax.nn.softplus(-lam)
    mult = jnp.sqrt(-jnp.expm1(2.0 * log_a))
    mult = jnp.where(jnp.arange(T)[None, :, None] == 0, 1.0, mult)
    u = mult * (i * y)

    def combine(left, right):
        a1, b1 = left
        a2, b2 = right
        return a1 * a2, a2 * b1 + b2

    _, h = lax.associative_scan(combine, (jnp.exp(log_a), u), axis=1)
    return h


def setup_inputs(seed: int = 0) -> dict:
    key = jax.random.key(seed)
    ks = jax.random.split(key, 29)
    f32 = jnp.float32
    nrm = lambda i, shape, s: s * jax.random.normal(ks[i], shape, f32)
    a_c = jax.random.uniform(ks[25], (DEPTH, BW), f32, 0.9, 0.999)
    a_base = a_c ** (1.0 / LRU_C)
    return {
        "x": nrm(0, (BATCH, SEQ, D_MODEL), 1.0),
        "c": nrm(1, (BATCH, D_MODEL), 1.0),
        "norm_g": 1.0 + nrm(2, (DEPTH, D_MODEL), 0.05),
        "w_mod": nrm(3, (DEPTH, D_MODEL, 3 * D_MODEL), 0.5 * D_MODEL ** -0.5),
        "b_mod": nrm(4, (DEPTH, 3 * D_MODEL), 0.02),
        "w_in": nrm(5, (DEPTH, D_MODEL, IN_WIDTH), D_MODEL ** -0.5),
        "rwkv_mu": jax.random.uniform(ks[6], (DEPTH, RWKV_SHIFT_WIDTH), f32),
        "rwkv_w0": jax.random.uniform(ks[7], (DEPTH, BW), f32, -6.0, -1.0),
        "rwkv_w2": nrm(8, (DEPTH, RWKV_DECAY_LORA, BW), 0.1 * RWKV_DECAY_LORA ** -0.5),
        "rwkv_a0": nrm(9, (DEPTH, BW), 0.1),
        "rwkv_a2": nrm(10, (DEPTH, RWKV_A_LORA, BW), 0.5 * RWKV_A_LORA ** -0.5),
        "rwkv_kk": 0.85 + nrm(11, (DEPTH, BW), 0.05),
        "rwkv_ka": 1.0 + nrm(12, (DEPTH, BW), 0.05),
        "rwkv_rk": nrm(13, (DEPTH, BW), 0.1),
        "rwkv_ln_g": 1.0 + nrm(14, (DEPTH, BW), 0.05),
        "rwkv_ln_b": nrm(15, (DEPTH, BW), 0.02),
        "hgrn_lb": nrm(16, (DEPTH, BW), 1.0),
        "hgrn_norm_g": 1.0 + nrm(17, (DEPTH, BW), 0.05),
        "ret_norm_g": 1.0 + nrm(18, (DEPTH, BW), 0.05),
        "lru_conv_w": nrm(19, (DEPTH, CONV_WIDTH, BW), CONV_WIDTH ** -0.5),
        "lru_conv_b": nrm(20, (DEPTH, BW), 0.02),
        "lru_wa": nrm(21, (DEPTH, LRU_BLOCKS, LRU_BLOCK, LRU_BLOCK), LRU_BLOCK ** -0.5),
        "lru_ba": nrm(22, (DEPTH, BW), 0.02),
        "lru_wx": nrm(23, (DEPTH, LRU_BLOCKS, LRU_BLOCK, LRU_BLOCK), LRU_BLOCK ** -0.5),
        "lru_bx": nrm(24, (DEPTH, BW), 0.02),
        "lru_lam": jnp.log(a_base) - jnp.log1p(-a_base),
        "w_branch": nrm(26, (DEPTH, N_BRANCH, BW, D_MODEL), BW ** -0.5),
        "w_out": nrm(27, (DEPTH, D_MODEL, D_MODEL), D_MODEL ** -0.5),
        "final_g": 1.0 + nrm(28, (D_MODEL,), 0.05),
    }


def reference(x, c, norm_g, w_mod, b_mod, w_in, rwkv_mu, rwkv_w0, rwkv_w2, rwkv_a0, rwkv_a2,
              rwkv_kk, rwkv_ka, rwkv_rk, rwkv_ln_g, rwkv_ln_b, hgrn_lb, hgrn_norm_g, ret_norm_g,
              lru_conv_w, lru_conv_b, lru_wa, lru_ba, lru_wx, lru_bx, lru_lam, w_branch, w_out,
              final_g):
    B, T, _ = x.shape
    split_at = np.cumsum(IN_SPLITS)[:-1].tolist()
    lb_p = jax.nn.softmax(hgrn_lb.astype(jnp.float32), axis=0)
    lb_all = jnp.cumsum(lb_p, axis=0) - lb_p[0]
    cond = jax.nn.silu(c.astype(jnp.float32))
    h = x
    for l in range(DEPTH):
        mod = cond @ w_mod[l] + b_mod[l]
        shift, scale, gate = jnp.split(mod, 3, -1)
        hn = _rmsnorm(h, norm_g[l]) * (1.0 + scale[:, None, :]) + shift[:, None, :]
        u = hn @ w_in[l]
        (ar, ak, av, aw, aa, ag, bq, bf, bi, bg, cq, ck, cv, cg, dx, dg, mg) = jnp.split(u, split_at, -1)
        ya = _rwkv7(ar, ak, av, aw, aa, rwkv_mu[l], rwkv_w0[l], rwkv_w2[l], rwkv_a0[l], rwkv_a2[l],
                    rwkv_kk[l], rwkv_ka[l], rwkv_rk[l], rwkv_ln_g[l], rwkv_ln_b[l]) * jax.nn.silu(ag)
        yb = _hgrn2(bq, bf, bi, lb_all[l], hgrn_norm_g[l]) * jax.nn.silu(bg)
        yc = _retention(cq, ck, cv, ret_norm_g[l]) * jax.nn.silu(cg)
        yd = _rglru(dx, lru_conv_w[l], lru_conv_b[l], lru_wa[l], lru_ba[l], lru_wx[l], lru_bx[l],
                    lru_lam[l]) * jax.nn.silu(dg)
        merge_gates = jax.nn.sigmoid(mg.astype(jnp.float32).reshape(B, T, N_BRANCH, D_MODEL))
        branches = (ya, yb, yc, yd)
        merged = sum(merge_gates[:, :, b_i] * (branches[b_i] @ w_branch[l, b_i]) for b_i in range(N_BRANCH))
        h = h + gate[:, None, :] * (merged @ w_out[l])
    return _rmsnorm(h, final_g)
```

```python
import functools

import numpy as np
import jax
import jax.numpy as jnp
from jax import lax
from jax.experimental import pallas as pl
from jax.experimental.pallas import tpu as pltpu

F32 = jnp.float32
BF16 = jnp.bfloat16

N_BRANCH = 4
BW = 256
N_HEAD = 4
HEAD = BW // N_HEAD
EPS = 1e-6
RWKV_LORA = 64
RWKV_SHIFT = 3 * BW + 2 * RWKV_LORA
RWKV_W = RWKV_SHIFT + BW
RWKV_LN_EPS = 64e-5
HGRN_W = 4 * BW
RET_QKHEAD = HEAD // 2
RET_QK = N_HEAD * RET_QKHEAD
RET_W = 2 * RET_QK + 2 * BW
ROPE_BASE = 10000.0
LRU_W = 2 * BW
CONV_WIDTH = 4
LRU_C = 8.0
MIX_W = RWKV_W + HGRN_W + RET_W + LRU_W

CHUNK = 64
N_LEVEL = 6
TOKEN_TILE = 256
VMEM_LIMIT = 48 * 1024 * 1024


def _dot(a, b):
    return jnp.dot(a.astype(BF16), b.astype(BF16), preferred_element_type=F32)


def _dot_nt(a, b):
    return lax.dot_general(a.astype(BF16), b.astype(BF16), (((1,), (1,)), ((), ())),
                           preferred_element_type=F32)


def _dot_tn(a, b):
    return lax.dot_general(a.astype(BF16), b.astype(BF16), (((0,), (0,)), ((), ())),
                           preferred_element_type=F32)


def _split2(x):
    hi = x.astype(BF16)
    lo = (x - hi.astype(F32)).astype(BF16)
    return hi, lo


def _split3(x):
    x1 = x.astype(BF16)
    r1 = x - x1.astype(F32)
    x2 = r1.astype(BF16)
    x3 = (r1 - x2.astype(F32)).astype(BF16)
    return x1, x2, x3


def _dot_hi(a, b):
    ah, al = _split2(a)
    bh, bl = _split2(b)
    d = functools.partial(jnp.dot, preferred_element_type=F32)
    return d(ah, bh) + (d(ah, bl) + d(al, bh))


def _dot01_left(m01, x):
    d = functools.partial(jnp.dot, preferred_element_type=F32)
    x1, x2, x3 = _split3(x)
    return d(m01, x1) + (d(m01, x2) + d(m01, x3))


def _dot01_right(x, m01):
    d = functools.partial(jnp.dot, preferred_element_type=F32)
    hi, lo = _split2(x)
    return d(hi, m01) + d(lo, m01)


def _sigmoid(x):
    return jax.nn.sigmoid(x)


def _silu(x):
    return x * jax.nn.sigmoid(x)


def _softplus(x):
    return jnp.maximum(x, 0.0) + jnp.log1p(jnp.exp(-jnp.abs(x)))


def _adaln(x, g, mod):
    ms = jnp.mean(x * x, axis=-1, keepdims=True)
    y = x * lax.rsqrt(ms + EPS) * g
    return y * (1.0 + mod[1:2, :]) + mod[0:1, :]


def _shift_rows(x, k, tail):
    xr = pltpu.roll(x, k, axis=0)
    row = lax.broadcasted_iota(jnp.int32, (8, x.shape[1]), 0)
    head = jnp.where(row < k, pltpu.roll(tail, k, axis=0), xr[0:8, :])
    return jnp.concatenate([head, xr[8:, :]], axis=0)


def _const_spec(shape):
    nd = len(shape)
    return pl.BlockSpec(shape, lambda *_: (0,) * nd)


def _layer_spec(layer, shape):
    nd = len(shape)
    return pl.BlockSpec((None,) + tuple(shape), lambda *_: (layer,) + (0,) * nd)


def _tok_spec(tm, width):
    return pl.BlockSpec((None, tm, width), lambda b, t: (b, t, 0))


def _mixer_params():
    return pltpu.CompilerParams(dimension_semantics=("parallel", "arbitrary"),
                                vmem_limit_bytes=VMEM_LIMIT)


def _mod_kernel(c_ref, w_ref, b_ref, o_ref):
    o_ref[...] = _dot_hi(_silu(c_ref[...]), w_ref[...]) + b_ref[...]


def _modulation(c, w_mod, b_mod):
    depth, d, d3 = w_mod.shape
    nb = c.shape[0]
    n_col = d3 // d
    return pl.pallas_call(
        _mod_kernel,
        grid=(depth, n_col),
        in_specs=[pl.BlockSpec((nb, d), lambda l, j: (0, 0)),
                  pl.BlockSpec((None, d, d), lambda l, j: (l, 0, j)),
                  pl.BlockSpec((None, 1, d), lambda l, j: (l, 0, j))],
        out_specs=pl.BlockSpec((None, nb, d), lambda l, j: (l, 0, j)),
        out_shape=jax.ShapeDtypeStruct((depth, nb, d3), F32),
        name="modulation",
    )(c, w_mod, b_mod.reshape(depth, 1, d3))


def _inproj_kernel(x_ref, mod_ref, g_ref, w_ref, oa_ref, ob_ref, oc_ref, od_ref):
    hn = _adaln(x_ref[...], g_ref[...], mod_ref[...]).astype(BF16)
    w = w_ref
    o = 0
    for ref in (oa_ref, ob_ref, oc_ref, od_ref):
        n = ref.shape[-1]
        ref[...] = jnp.dot(hn, w[:, o:o + n], preferred_element_type=F32)
        o += n


def _inproj(layer, h, mod, norm_g, w_mix):
    nb, nt, d = h.shape
    tm = min(TOKEN_TILE, nt)
    widths = (RWKV_W, HGRN_W, RET_W, LRU_W)
    return pl.pallas_call(
        _inproj_kernel,
        grid=(nb, nt // tm),
        in_specs=[_tok_spec(tm, d),
                  pl.BlockSpec((None, None, 3, d), lambda b, t: (layer, b, 0, 0)),
                  _layer_spec(layer, (1, d)),
                  _layer_spec(layer, (d, MIX_W))],
        out_specs=[_tok_spec(tm, n) for n in widths],
        out_shape=[jax.ShapeDtypeStruct((nb, nt, n), F32) for n in widths],
        compiler_params=pltpu.CompilerParams(dimension_semantics=("parallel", "parallel"),
                                             vmem_limit_bytes=VMEM_LIMIT),
        name="inproj",
    )(h, mod, norm_g, w_mix)


def _unit_lower_inverse(a):
    n = a.shape[0]
    eye = (lax.broadcasted_iota(jnp.int32, (n, n), 0)
           == lax.broadcasted_iota(jnp.int32, (n, n), 1)).astype(F32)
    inv = eye + a
    x = a
    for _ in range(N_LEVEL - 1):
        x = _dot_hi(x, x)
        inv = inv + _dot_hi(inv, x)
    return inv


def _rwkv_kernel(u_ref, mu_ref, w0_ref, w2_ref, a0_ref, a2_ref, kk_ref, ka_ref, rk_ref,
                 lng_ref, lnb_ref, tril_ref, hsum_ref, o_ref, tail_ref, s_ref):
    tm = u_ref.shape[0]

    @pl.when(pl.program_id(1) == 0)
    def _():
        tail_ref[...] = jnp.zeros_like(tail_ref)
        s_ref[...] = jnp.zeros_like(s_ref)

    u = u_ref[...]
    feats = u[:, :RWKV_SHIFT]
    gate = u[:, RWKV_SHIFT:]
    prev = _shift_rows(feats, 1, tail_ref[...])
    tail_ref[...] = feats[tm - 8:, :]
    x = feats + mu_ref[...] * (prev - feats)
    r = x[:, 0:BW]
    k = x[:, BW:2 * BW]
    v = x[:, 2 * BW:3 * BW]
    w_lo = x[:, 3 * BW:3 * BW + RWKV_LORA]
    a_lo = x[:, 3 * BW + RWKV_LORA:]

    hsum = hsum_ref[...]
    w = -_softplus(-(w0_ref[...] + _dot(jnp.tanh(w_lo), w2_ref[...]))) - 0.5
    log_w = -jnp.exp(w)
    a = _sigmoid(a0_ref[...] + _dot(a_lo, a2_ref[...]))
    kk = k * kk_ref[...]
    k = k * (1.0 + (a - 1.0) * ka_ref[...])
    kk = kk / jnp.maximum(jnp.sqrt(_dot01_right(kk * kk, hsum)), 1e-12)
    beta = kk * a

    c = _dot01_left(tril_ref[...], log_w)
    row = lax.broadcasted_iota(jnp.int32, (CHUNK, CHUNK), 0)
    col = lax.broadcasted_iota(jnp.int32, (CHUNK, CHUNK), 1)
    strict = row > col
    lower = row >= col
    ys = []
    for j in range(tm // CHUNK):
        sl = slice(j * CHUNK, (j + 1) * CHUNK)
        cj = c[sl]
        c_last = cj[CHUNK - 1:CHUNK, :]
        e_in = jnp.exp(cj)
        e_prev = jnp.exp(cj - log_w[sl])
        e_out = jnp.exp(-cj)
        e_rest = jnp.exp(c_last - cj)
        e_last = jnp.exp(c_last)
        al_t = -kk[sl] * e_prev
        r_t = r[sl] * e_in
        b_t = beta[sl] * e_out
        k_t = k[sl] * e_out
        b_e = beta[sl] * e_rest
        k_e = k[sl] * e_rest
        vj = v[sl]
        outs = []
        for h in range(N_HEAD):
            hs = slice(h * HEAD, (h + 1) * HEAD)
            lhs = jnp.concatenate([al_t[:, hs], r_t[:, hs]], axis=0)
            rhs = jnp.concatenate([b_t[:, hs], k_t[:, hs]], axis=0)
            p = _dot_nt(lhs, rhs)
            a_ab = jnp.where(strict, p[:CHUNK, :CHUNK], 0.0)
            a_ak = jnp.where(strict, p[:CHUNK, CHUNK:], 0.0)
            a_rb = jnp.where(lower, p[CHUNK:, :CHUNK], 0.0)
            a_rk = jnp.where(lower, p[CHUNK:, CHUNK:], 0.0)
            inv = _unit_lower_inverse(a_ab)
            s = s_ref[h]
            ls = _dot_nt(lhs, s)
            vh = vj[:, hs]
            uu = _dot(inv, ls[:CHUNK] + _dot(a_ak, vh))
            uv = jnp.concatenate([uu, vh], axis=0)
            outs.append(ls[CHUNK:] + _dot(jnp.concatenate([a_rb, a_rk], axis=1), uv))
            s_ref[h] = s * e_last[:, hs] + _dot_tn(
                uv, jnp.concatenate([b_e[:, hs], k_e[:, hs]], axis=0))
        ys.append(jnp.concatenate(outs, axis=1))
    y = jnp.concatenate(ys, axis=0)

    mean = _dot01_right(y, hsum) * (1.0 / HEAD)
    yc = y - mean
    var = _dot01_right(yc * yc, hsum) * (1.0 / HEAD)
    y = yc * lax.rsqrt(var + RWKV_LN_EPS) * lng_ref[...] + lnb_ref[...]
    bonus = _dot01_right(r * k * rk_ref[...], hsum) * v
    o_ref[...] = (y + bonus) * _silu(gate)


def _rwkv(layer, u, p, tril, hsum):
    nb, nt, _ = u.shape
    tm = min(TOKEN_TILE, nt)
    vec = lambda n: _layer_spec(layer, (1, n))
    return pl.pallas_call(
        _rwkv_kernel,
        grid=(nb, nt // tm),
        in_specs=[_tok_spec(tm, RWKV_W), vec(RWKV_SHIFT), vec(BW),
                  _layer_spec(layer, (RWKV_LORA, BW)), vec(BW),
                  _layer_spec(layer, (RWKV_LORA, BW)), vec(BW), vec(BW), vec(BW), vec(BW), vec(BW),
                  _const_spec(tril.shape), _const_spec(hsum.shape)],
        out_specs=_tok_spec(tm, BW),
        out_shape=jax.ShapeDtypeStruct((nb, nt, BW), F32),
        scratch_shapes=[pltpu.VMEM((8, RWKV_SHIFT), F32), pltpu.VMEM((N_HEAD, HEAD, HEAD), F32)],
        compiler_params=_mixer_params(),
        name="rwkv7",
    )(u, p["mu"], p["w0"], p["w2"], p["a0"], p["a2"], p["kk"], p["ka"], p["rk"], p["ln_g"],
      p["ln_b"], tril, hsum)


def _hgrn_level_matrices():
    c = CHUNK
    t = np.arange(c)[:, None]
    j = np.arange(c)[None, :]
    mats = [(j <= t)]
    for lvl in range(N_LEVEL):
        blk = c >> lvl
        mid = (t // blk) * blk + blk // 2
        upper = (t % blk) >= blk // 2
        mats.append(np.where(upper, (j >= mid) & (j <= t), (j > t) & (j < mid)))
    return jnp.asarray(np.concatenate(mats, axis=0).astype(np.float32), dtype=BF16)


def _hgrn_kernel(u_ref, lb_ref, g_ref, lvl_ref, hsum_ref, o_ref, s_ref):
    tm = u_ref.shape[0]

    @pl.when(pl.program_id(1) == 0)
    def _():
        s_ref[...] = jnp.zeros_like(s_ref)

    u = u_ref[...]
    q = u[:, 0:BW]
    kx = (1.0 - lb_ref[...]) * _sigmoid(-u[:, BW:2 * BW])
    log_f = jnp.log1p(-kx)
    v = u[:, 2 * BW:3 * BW]
    gate = u[:, 3 * BW:]

    row = lax.broadcasted_iota(jnp.int32, (CHUNK, CHUNK), 0)
    col = lax.broadcasted_iota(jnp.int32, (CHUNK, CHUNK), 1)
    masks = []
    for lvl in range(N_LEVEL):
        sh = N_LEVEL - lvl
        half = 1 << (sh - 1)
        same = (row >> sh) == (col >> sh)
        masks.append(same & ((row & half) != 0) & ((col & half) == 0))
    diag = row == col

    ys = []
    for j in range(tm // CHUNK):
        sl = slice(j * CHUNK, (j + 1) * CHUNK)
        qj, kj, vj = q[sl], kx[sl], v[sl]
        d_all = _dot01_left(lvl_ref[...], log_f[sl])
        b = d_all[:CHUNK]
        b_last = b[CHUNK - 1:CHUNK, :]
        q_in = qj * jnp.exp(b)
        k_e = kj * jnp.exp(b_last - b)
        e_last = jnp.exp(b_last)
        q_l, k_l = [], []
        for lvl in range(N_LEVEL):
            e = jnp.exp(d_all[(lvl + 1) * CHUNK:(lvl + 2) * CHUNK])
            q_l.append(qj * e)
            k_l.append(kj * e)
        outs = []
        for h in range(N_HEAD):
            hs = slice(h * HEAD, (h + 1) * HEAD)
            sc = jnp.where(diag, _dot_nt(qj[:, hs], kj[:, hs]), 0.0)
            for lvl in range(N_LEVEL):
                sc = jnp.where(masks[lvl], _dot_nt(q_l[lvl][:, hs], k_l[lvl][:, hs]), sc)
            s = s_ref[h]
            outs.append(_dot(sc, vj[:, hs]) + _dot_nt(q_in[:, hs], s))
            s_ref[h] = s * e_last[:, hs] + _dot_tn(vj[:, hs], k_e[:, hs])
        ys.append(jnp.concatenate(outs, axis=1))
    y = jnp.concatenate(ys, axis=0)
    ms = _dot01_right(y * y, hsum_ref[...]) * (1.0 / HEAD)
    o_ref[...] = y * lax.rsqrt(ms + EPS) * g_ref[...] * _silu(gate)


def _hgrn(layer, u, lb_all, norm_g, lvl, hsum):
    nb, nt, _ = u.shape
    tm = min(TOKEN_TILE, nt)
    return pl.pallas_call(
        _hgrn_kernel,
        grid=(nb, nt // tm),
        in_specs=[_tok_spec(tm, HGRN_W), _layer_spec(layer, (1, BW)), _layer_spec(layer, (1, BW)),
                  _const_spec(lvl.shape), _const_spec(hsum.shape)],
        out_specs=_tok_spec(tm, BW),
        out_shape=jax.ShapeDtypeStruct((nb, nt, BW), F32),
        scratch_shapes=[pltpu.VMEM((N_HEAD, HEAD, HEAD), F32)],
        compiler_params=_mixer_params(),
        name="hgrn2",
    )(u, lb_all, norm_g, lvl, hsum)


def _ret_tables(nt, tm):
    dk = RET_QKHEAD
    pos = jnp.arange(nt, dtype=F32)
    inv_freq = 1.0 / (ROPE_BASE ** jnp.linspace(0.0, 1.0, dk // 2, dtype=F32))
    ang = pos[:, None] * inv_freq[None, :]
    cos, sin = jnp.cos(ang), jnp.sin(ang)
    cos_t = jnp.tile(jnp.concatenate([cos, cos], -1), (1, N_HEAD))
    sin_t = jnp.tile(jnp.concatenate([-sin, sin], -1), (1, N_HEAD))
    log_gamma = jnp.log1p(-jnp.exp2(-5.0 - jnp.arange(N_HEAD, dtype=F32)))
    idx = jnp.arange(tm, dtype=F32)
    rel = idx[:, None] - idx[None, :]
    decay = jnp.where(rel >= 0, jnp.exp(log_gamma[:, None, None] * jnp.maximum(rel, 0.0)), 0.0)
    zeta = jnp.exp(log_gamma[:, None] * (tm - 1 - idx))
    xi = jnp.exp(log_gamma[:, None] * (idx + 1.0))
    gamma_c = jnp.exp(log_gamma * tm)
    lanes = lambda t: jnp.repeat(t.T, dk, axis=1)
    return cos_t, sin_t, decay, lanes(zeta), lanes(xi), lanes(gamma_c[:, None])


def _ret_kernel(u_ref, cos_ref, sin_ref, dec_ref, zeta_ref, xi_ref, gam_ref, g_ref, hsum_ref,
                o_ref, s_ref):
    @pl.when(pl.program_id(1) == 0)
    def _():
        s_ref[...] = jnp.zeros_like(s_ref)

    u = u_ref[...]
    cos, sin = cos_ref[...], sin_ref[...]
    lane = lax.broadcasted_iota(jnp.int32, cos.shape, 1)
    first_half = (lane & (RET_QKHEAD // 2)) == 0

    def rope(t):
        half = RET_QKHEAD // 2
        swapped = jnp.where(first_half, pltpu.roll(t, RET_QK - half, axis=1),
                            pltpu.roll(t, half, axis=1))
        return t * cos + swapped * sin

    q = rope(u[:, 0:RET_QK])
    k = rope(u[:, RET_QK:2 * RET_QK]) * (RET_QKHEAD ** -0.5)
    v = u[:, 2 * RET_QK:2 * RET_QK + BW]
    gate = u[:, 2 * RET_QK + BW:]
    q_x = q * xi_ref[...]
    k_z = k * zeta_ref[...]
    gam = gam_ref[...]
    outs = []
    for h in range(N_HEAD):
        qs = slice(h * RET_QKHEAD, (h + 1) * RET_QKHEAD)
        hs = slice(h * HEAD, (h + 1) * HEAD)
        sc = _dot_nt(q[:, qs], k[:, qs]) * dec_ref[h]
        s = s_ref[h]
        outs.append(_dot(sc, v[:, hs]) + _dot_nt(q_x[:, qs], s))
        s_ref[h] = s * gam[:, qs] + _dot_tn(v[:, hs], k_z[:, qs])
    y = jnp.concatenate(outs, axis=1)
    ms = _dot01_right(y * y, hsum_ref[...]) * (1.0 / HEAD)
    o_ref[...] = y * lax.rsqrt(ms + EPS) * g_ref[...] * _silu(gate)


def _retention(layer, u, norm_g, tables, hsum):
    nb, nt, _ = u.shape
    tm = min(TOKEN_TILE, nt)
    cos_t, sin_t, decay, zeta, xi, gam = tables
    pos_spec = pl.BlockSpec((tm, RET_QK), lambda b, t: (t, 0))
    return pl.pallas_call(
        _ret_kernel,
        grid=(nb, nt // tm),
        in_specs=[_tok_spec(tm, RET_W), pos_spec, pos_spec, _const_spec(decay.shape),
                  _const_spec(zeta.shape), _const_spec(xi.shape), _const_spec(gam.shape),
                  _layer_spec(layer, (1, BW)), _const_spec(hsum.shape)],
        out_specs=_tok_spec(tm, BW),
        out_shape=jax.ShapeDtypeStruct((nb, nt, BW), F32),
        scratch_shapes=[pltpu.VMEM((N_HEAD, HEAD, RET_QKHEAD), F32)],
        compiler_params=_mixer_params(),
        name="retention",
    )(u, cos_t, sin_t, decay, zeta, xi, gam, norm_g, hsum)


def _lru_kernel(u_ref, cw_ref, cb_ref, wa_ref, ba_ref, wx_ref, bx_ref, lam_ref, o_ref,
                tail_ref, h_ref):
    tm = u_ref.shape[0]
    first = pl.program_id(1) == 0

    @pl.when(first)
    def _():
        tail_ref[...] = jnp.zeros_like(tail_ref)
        h_ref[...] = jnp.zeros_like(h_ref)

    u = u_ref[...]
    x = u[:, :BW]
    gate = u[:, BW:]
    tail = tail_ref[...]
    cw = cw_ref[...]
    y = cb_ref[...] + cw[CONV_WIDTH - 1:CONV_WIDTH, :] * x
    for k in range(1, CONV_WIDTH):
        y = y + cw[CONV_WIDTH - 1 - k:CONV_WIDTH - k, :] * _shift_rows(x, k, tail)
    tail_ref[...] = x[tm - 8:, :]

    r = _sigmoid(_dot(y, wa_ref[...]) + ba_ref[...])
    i = _sigmoid(_dot(y, wx_ref[...]) + bx_ref[...])
    log_a = -LRU_C * r * _softplus(-lam_ref[...])
    a = jnp.exp(log_a)
    mult = jnp.sqrt(jnp.tanh(-log_a) * (a * a + 1.0))
    row = lax.broadcasted_iota(jnp.int32, x.shape, 0)
    mult = jnp.where(row + pl.program_id(1) * tm == 0, 1.0, mult)
    b = mult * (i * y)

    d = 1
    while d < tm:
        a_sh = jnp.where(row < d, 1.0, pltpu.roll(a, d, axis=0))
        b_sh = jnp.where(row < d, 0.0, pltpu.roll(b, d, axis=0))
        b = a * b_sh + b
        a = a * a_sh
        d *= 2
    h = a * h_ref[0:1, :] + b
    h_ref[0:1, :] = h[tm - 1:tm, :]
    o_ref[...] = h * _silu(gate)


def _rglru(layer, u, p):
    nb, nt, _ = u.shape
    tm = min(TOKEN_TILE, nt)
    vec = lambda: _layer_spec(layer, (1, BW))
    mat = lambda: _layer_spec(layer, (BW, BW))
    return pl.pallas_call(
        _lru_kernel,
        grid=(nb, nt // tm),
        in_specs=[_tok_spec(tm, LRU_W), _layer_spec(layer, (CONV_WIDTH, BW)), vec(), mat(), vec(),
                  mat(), vec(), vec()],
        out_specs=_tok_spec(tm, BW),
        out_shape=jax.ShapeDtypeStruct((nb, nt, BW), F32),
        scratch_shapes=[pltpu.VMEM((8, BW), F32), pltpu.VMEM((8, BW), F32)],
        compiler_params=_mixer_params(),
        name="rglru",
    )(u, p["conv_w"], p["conv_b"], p["wa"], p["ba"], p["wx"], p["bx"], p["lam"])


def _merge_kernel(h_ref, mod_ref, g_ref, ya_ref, yb_ref, yc_ref, yd_ref, wg_ref, wb_ref, wo_ref,
                  fg_ref, o_ref, *, final_norm):
    x = h_ref[...]
    d = x.shape[-1]
    mod = mod_ref[...]
    hn = _adaln(x, g_ref[...], mod).astype(BF16)
    merged = None
    for bi, y_ref in enumerate((ya_ref, yb_ref, yc_ref, yd_ref)):
        gates = _sigmoid(jnp.dot(hn, wg_ref[:, bi * d:(bi + 1) * d], preferred_element_type=F32))
        term = gates * _dot(y_ref[...], wb_ref[bi])
        merged = term if merged is None else merged + term
    out = x + mod[2:3, :] * _dot(merged, wo_ref[...])
    if final_norm:
        out = out * lax.rsqrt(jnp.mean(out * out, axis=-1, keepdims=True) + EPS) * fg_ref[...]
    o_ref[...] = out


def _merge(layer, h, mod, norm_g, ys, w_gate, w_branch, w_out, final_g, final_norm):
    nb, nt, d = h.shape
    tm = min(TOKEN_TILE, nt)
    return pl.pallas_call(
        functools.partial(_merge_kernel, final_norm=final_norm),
        grid=(nb, nt // tm),
        in_specs=[_tok_spec(tm, d),
                  pl.BlockSpec((None, None, 3, d), lambda b, t: (layer, b, 0, 0)),
                  _layer_spec(layer, (1, d))]
                 + [_tok_spec(tm, BW)] * N_BRANCH
                 + [_layer_spec(layer, (d, N_BRANCH * d)), _layer_spec(layer, (N_BRANCH, BW, d)),
                    _layer_spec(layer, (d, d)), _const_spec((1, d))],
        out_specs=_tok_spec(tm, d),
        out_shape=jax.ShapeDtypeStruct((nb, nt, d), F32),
        compiler_params=pltpu.CompilerParams(dimension_semantics=("parallel", "parallel"),
                                             vmem_limit_bytes=VMEM_LIMIT),
        name="merge",
    )(h, mod, norm_g, *ys, w_gate, w_branch, w_out, final_g)


def _block_diag(w):
    depth, g, n, _ = w.shape
    eye = jnp.eye(g, dtype=w.dtype)
    return (w[:, :, :, None, :] * eye[None, :, None, :, None]).reshape(depth, g * n, g * n)


def kernel(x, c, norm_g, w_mod, b_mod, w_in, rwkv_mu, rwkv_w0, rwkv_w2, rwkv_a0, rwkv_a2, rwkv_kk, rwkv_ka, rwkv_rk, rwkv_ln_g, rwkv_ln_b, hgrn_lb, hgrn_norm_g, ret_norm_g, lru_conv_w, lru_conv_b, lru_wa, lru_ba, lru_wx, lru_bx, lru_lam, w_branch, w_out, final_g):
    nb, nt, d = x.shape
    depth = w_in.shape[0]
    tm = min(TOKEN_TILE, nt)
    assert nt % tm == 0 and tm % CHUNK == 0 and w_in.shape[2] == MIX_W + N_BRANCH * d

    row3 = lambda p: p.reshape(depth, 1, -1)
    w_mix = w_in[:, :, :MIX_W].astype(BF16)
    w_gate = w_in[:, :, MIX_W:].astype(BF16)
    w_branch_b = w_branch.astype(BF16)
    w_out_b = w_out.astype(BF16)
    rwkv_p = dict(mu=row3(rwkv_mu), w0=row3(rwkv_w0), w2=rwkv_w2.astype(BF16), a0=row3(rwkv_a0),
                  a2=rwkv_a2.astype(BF16), kk=row3(rwkv_kk), ka=row3(rwkv_ka), rk=row3(rwkv_rk),
                  ln_g=row3(rwkv_ln_g), ln_b=row3(rwkv_ln_b))
    lru_p = dict(conv_w=lru_conv_w, conv_b=row3(lru_conv_b), wa=_block_diag(lru_wa).astype(BF16),
                 ba=row3(lru_ba), wx=_block_diag(lru_wx).astype(BF16), bx=row3(lru_bx),
                 lam=row3(lru_lam))
    lb_p = jax.nn.softmax(hgrn_lb.astype(F32), axis=0)
    lb_all = row3(jnp.cumsum(lb_p, axis=0) - lb_p[0])

    head_id = np.arange(BW) // HEAD
    hsum = jnp.asarray((head_id[:, None] == head_id[None, :]).astype(np.float32), dtype=BF16)
    tok = np.arange(tm)
    tril = jnp.asarray(((tok[:, None] // CHUNK == tok[None, :] // CHUNK)
                        & (tok[None, :] <= tok[:, None])).astype(np.float32), dtype=BF16)
    lvl = _hgrn_level_matrices()
    ret_tab = _ret_tables(nt, tm)

    mod = _modulation(c, w_mod, b_mod).reshape(depth, nb, 3, d)
    g3 = row3(norm_g)
    h = x
    for l in range(depth):
        ua, ub, uc, ud = _inproj(l, h, mod, g3, w_mix)
        ya = _rwkv(l, ua, rwkv_p, tril, hsum)
        yb = _hgrn(l, ub, lb_all, row3(hgrn_norm_g), lvl, hsum)
        yc = _retention(l, uc, row3(ret_norm_g), ret_tab, hsum)
        yd = _rglru(l, ud, lru_p)
        h = _merge(l, h, mod, g3, (ya, yb, yc, yd), w_gate, w_branch_b, w_out_b,
                   final_g.reshape(1, d), final_norm=(l == depth - 1))
    return h
```

```python
import functools

import numpy as np
import jax
import jax.numpy as jnp
from jax import lax
from jax.experimental import pallas as pl
from jax.experimental.pallas import tpu as pltpu

F32 = jnp.float32
BF16 = jnp.bfloat16

N_BRANCH = 4
BW = 256
N_HEAD = 4
HEAD = BW // N_HEAD
EPS = 1e-6
RWKV_LORA = 64
RWKV_SHIFT = 3 * BW + 2 * RWKV_LORA
RWKV_W = RWKV_SHIFT + BW
RWKV_LN_EPS = 64e-5
HGRN_W = 4 * BW
RET_QKHEAD = HEAD // 2
RET_QK = N_HEAD * RET_QKHEAD
RET_W = 2 * RET_QK + 2 * BW
ROPE_BASE = 10000.0
LRU_W = 2 * BW
CONV_WIDTH = 4
LRU_C = 8.0
MIX_W = RWKV_W + HGRN_W + RET_W + LRU_W

CHUNK = 64
N_LEVEL = 6
TOKEN_TILE = 256
VMEM_LIMIT = 48 * 1024 * 1024


def _dot(a, b):
    return jnp.dot(a.astype(BF16), b.astype(BF16), preferred_element_type=F32)


def _dot_nt(a, b):
    return lax.dot_general(a.astype(BF16), b.astype(BF16), (((1,), (1,)), ((), ())),
                           preferred_element_type=F32)


def _dot_tn(a, b):
    return lax.dot_general(a.astype(BF16), b.astype(BF16), (((0,), (0,)), ((), ())),
                           preferred_element_type=F32)


def _split2(x):
    hi = x.astype(BF16)
    lo = (x - hi.astype(F32)).astype(BF16)
    return hi, lo


def _split3(x):
    x1 = x.astype(BF16)
    r1 = x - x1.astype(F32)
    x2 = r1.astype(BF16)
    x3 = (r1 - x2.astype(F32)).astype(BF16)
    return x1, x2, x3


def _dot_hi(a, b):
    ah, al = _split2(a)
    bh, bl = _split2(b)
    d = functools.partial(jnp.dot, preferred_element_type=F32)
    return d(ah, bh) + (d(ah, bl) + d(al, bh))


def _dot01_left(m01, x):
    d = functools.partial(jnp.dot, preferred_element_type=F32)
    x1, x2, x3 = _split3(x)
    return d(m01, x1) + (d(m01, x2) + d(m01, x3))


def _dot01_right(x, m01):
    d = functools.partial(jnp.dot, preferred_element_type=F32)
    hi, lo = _split2(x)
    return d(hi, m01) + d(lo, m01)


def _sigmoid(x):
    return jax.nn.sigmoid(x)


def _silu(x):
    return x * jax.nn.sigmoid(x)


def _softplus(x):
    return jnp.maximum(x, 0.0) + jnp.log1p(jnp.exp(-jnp.abs(x)))


def _adaln(x, g, mod):
    ms = jnp.mean(x * x, axis=-1, keepdims=True)
    y = x * lax.rsqrt(ms + EPS) * g
    return y * (1.0 + mod[1:2, :]) + mod[0:1, :]


def _shift_rows(x, k, tail):
    xr = pltpu.roll(x, k, axis=0)
    row = lax.broadcasted_iota(jnp.int32, (8, x.shape[1]), 0)
    head = jnp.where(row < k, pltpu.roll(tail, k, axis=0), xr[0:8, :])
    return jnp.concatenate([head, xr[8:, :]], axis=0)


def _const_spec(shape):
    nd = len(shape)
    return pl.BlockSpec(shape, lambda *_: (0,) * nd)


def _layer_spec(layer, shape):
    nd = len(shape)
    return pl.BlockSpec((None,) + tuple(shape), lambda *_: (layer,) + (0,) * nd)


def _tok_spec(tm, width):
    return pl.BlockSpec((None, tm, width), lambda b, t: (b, t, 0))


def _mixer_params():
    return pltpu.CompilerParams(dimension_semantics=("parallel", "arbitrary"),
                                vmem_limit_bytes=VMEM_LIMIT)


def _mod_kernel(c_ref, w_ref, b_ref, o_ref):
    o_ref[...] = _dot_hi(_silu(c_ref[...]), w_ref[...]) + b_ref[...]


def _modulation(c, w_mod, b_mod):
    depth, d, d3 = w_mod.shape
    nb = c.shape[0]
    n_col = d3 // d
    return pl.pallas_call(
        _mod_kernel,
        grid=(depth, n_col),
        in_specs=[pl.BlockSpec((nb, d), lambda l, j: (0, 0)),
                  pl.BlockSpec((None, d, d), lambda l, j: (l, 0, j)),
                  pl.BlockSpec((None, 1, d), lambda l, j: (l, 0, j))],
        out_specs=pl.BlockSpec((None, nb, d), lambda l, j: (l, 0, j)),
        out_shape=jax.ShapeDtypeStruct((depth, nb, d3), F32),
        name="modulation",
    )(c, w_mod, b_mod.reshape(depth, 1, d3))


def _inproj_kernel(x_ref, mod_ref, g_ref, w_ref, oa_ref, ob_ref, oc_ref, od_ref):
    hn = _adaln(x_ref[...], g_ref[...], mod_ref[...]).astype(BF16)
    w = w_ref
    o = 0
    for ref in (oa_ref, ob_ref, oc_ref, od_ref):
        n = ref.shape[-1]
        ref[...] = jnp.dot(hn, w[:, o:o + n], preferred_element_type=F32)
        o += n


def _inproj(layer, h, mod, norm_g, w_mix):
    nb, nt, d = h.shape
    tm = min(TOKEN_TILE, nt)
    widths = (RWKV_W, HGRN_W, RET_W, LRU_W)
    return pl.pallas_call(
        _inproj_kernel,
        grid=(nb, nt // tm),
        in_specs=[_tok_spec(tm, d),
                  pl.BlockSpec((None, None, 3, d), lambda b, t: (layer, b, 0, 0)),
                  _layer_spec(layer, (1, d)),
                  _layer_spec(layer, (d, MIX_W))],
        out_specs=[_tok_spec(tm, n) for n in widths],
        out_shape=[jax.ShapeDtypeStruct((nb, nt, n), F32) for n in widths],
        compiler_params=pltpu.CompilerParams(dimension_semantics=("parallel", "parallel"),
                                             vmem_limit_bytes=VMEM_LIMIT),
        name="inproj",
    )(h, mod, norm_g, w_mix)


def _block(x, bd01):
    xb = x.astype(BF16)
    return jnp.concatenate([xb] * N_HEAD, axis=0) * bd01


def _unit_lower_inverses(a_list, eye_wide, bd01):
    n = len(a_list)
    xs = [_dot(a, _block(a, bd01)) for a in a_list]
    invs = [eye_wide + a for a in a_list]
    for lvl in range(1, N_LEVEL):
        for i in range(n):
            xb = _block(xs[i], bd01)
            if lvl < N_LEVEL - 1:
                res = _dot(jnp.concatenate([xs[i], invs[i]], axis=0), xb)
                xs[i] = res[:CHUNK]
                invs[i] = invs[i] + res[CHUNK:]
            else:
                invs[i] = invs[i] + _dot(invs[i], xb)
    return invs


def _rwkv_kernel(u_ref, mu_ref, w0_ref, w2_ref, a0_ref, a2_ref, kk_ref, ka_ref, rk_ref,
                 lng_ref, lnb_ref, tril_ref, hsum_ref, o_ref, tail_ref, s_ref):
    tm = u_ref.shape[0]

    @pl.when(pl.program_id(1) == 0)
    def _():
        tail_ref[...] = jnp.zeros_like(tail_ref)
        s_ref[...] = jnp.zeros_like(s_ref)

    u = u_ref[...]
    feats = u[:, :RWKV_SHIFT]
    gate = u[:, RWKV_SHIFT:]
    prev = _shift_rows(feats, 1, tail_ref[...])
    tail_ref[...] = feats[tm - 8:, :]
    x = feats + mu_ref[...] * (prev - feats)
    r = x[:, 0:BW]
    k = x[:, BW:2 * BW]
    v = x[:, 2 * BW:3 * BW]
    w_lo = x[:, 3 * BW:3 * BW + RWKV_LORA]
    a_lo = x[:, 3 * BW + RWKV_LORA:]

    hsum = hsum_ref[...]
    w = -_softplus(-(w0_ref[...] + _dot(jnp.tanh(w_lo), w2_ref[...]))) - 0.5
    log_w = -jnp.exp(w)
    a = _sigmoid(a0_ref[...] + _dot(a_lo, a2_ref[...]))
    kk = k * kk_ref[...]
    k = k * (1.0 + (a - 1.0) * ka_ref[...])
    kk = kk / jnp.maximum(jnp.sqrt(_dot01_right(kk * kk, hsum)), 1e-12)
    beta = kk * a

    c = _dot01_left(tril_ref[...], log_w)
    n_chunk = tm // CHUNK
    bd01 = hsum
    row = lax.broadcasted_iota(jnp.int32, (CHUNK, BW), 0)
    pos = lax.broadcasted_iota(jnp.int32, (CHUNK, BW), 1) & (HEAD - 1)
    strict = row > pos
    lower = row >= pos
    eye_wide = (row == pos).astype(F32)
    same_head = ((lax.broadcasted_iota(jnp.int32, (BW, BW), 0) // HEAD)
                 == (lax.broadcasted_iota(jnp.int32, (BW, BW), 1) // HEAD))

    al_t, r_t, a_ab, a_ak, a_rb, a_rk, b_eT, kv_c, e_col, v_blk = ([] for _ in range(10))
    for j in range(n_chunk):
        sl = slice(j * CHUNK, (j + 1) * CHUNK)
        cj = c[sl]
        c_last = cj[CHUNK - 1:CHUNK, :]
        e_in = jnp.exp(cj)
        e_prev = jnp.exp(cj - log_w[sl])
        e_out = jnp.exp(-cj)
        e_rest = jnp.exp(c_last) * e_out
        al_t.append(-kk[sl] * e_prev)
        r_t.append(r[sl] * e_in)
        rhs = jnp.concatenate([_block(beta[sl] * e_out, bd01), _block(k[sl] * e_out, bd01)], axis=0)
        p = _dot_nt(jnp.concatenate([al_t[j], r_t[j]], axis=0), rhs)
        a_ab.append(jnp.where(strict, p[:CHUNK, :BW], 0.0))
        a_ak.append(jnp.where(strict, p[:CHUNK, BW:], 0.0))
        a_rb.append(jnp.where(lower, p[CHUNK:, :BW], 0.0))
        a_rk.append(jnp.where(lower, p[CHUNK:, BW:], 0.0))
        ends = jnp.concatenate([beta[sl] * e_rest, k[sl] * e_rest], axis=0).T
        b_eT.append(ends[:, :CHUNK])
        v_blk.append(_block(v[sl], bd01))
        kv_c.append(jnp.where(same_head, _dot(ends[:, CHUNK:], v[sl]), 0.0))
        col = jnp.exp(jnp.broadcast_to(c_last, (2 * CHUNK, BW)).T)
        e_col.append(jnp.concatenate([col, col], axis=1))
    invs = _unit_lower_inverses(a_ab, eye_wide, bd01)
    t_al, t_akv, q_c, z_c = [], [], [], []
    for j in range(n_chunk):
        av = _dot(jnp.concatenate([a_ak[j], a_rk[j]], axis=0), v_blk[j])
        tw = _dot(invs[j], jnp.concatenate([_block(al_t[j], bd01), _block(av[:CHUNK], bd01)], axis=1))
        t_al.append(tw[:, :BW])
        t_akv.append(tw[:, BW:])
        qz = _dot(a_rb[j], jnp.concatenate([_block(t_al[j], bd01), _block(t_akv[j], bd01)], axis=1))
        q_c.append(r_t[j] + qz[:, :BW])
        z_c.append(qz[:, BW:] + av[CHUNK:])

    s = s_ref[...]
    ys = []
    for j in range(n_chunk):
        uo = _dot(jnp.concatenate([t_al[j], q_c[j]], axis=0), s)
        ys.append(uo[CHUNK:] + z_c[j])
        uu = uo[:CHUNK] + t_akv[j]
        s = e_col[j] * s + jnp.where(same_head, _dot(b_eT[j], uu), 0.0) + kv_c[j]
    s_ref[...] = s
    y = jnp.concatenate(ys, axis=0)

    mean = _dot01_right(y, hsum) * (1.0 / HEAD)
    yc = y - mean
    var = _dot01_right(yc * yc, hsum) * (1.0 / HEAD)
    y = yc * lax.rsqrt(var + RWKV_LN_EPS) * lng_ref[...] + lnb_ref[...]
    bonus = _dot01_right(r * k * rk_ref[...], hsum) * v
    o_ref[...] = (y + bonus) * _silu(gate)


def _rwkv(layer, u, p, tril, hsum):
    nb, nt, _ = u.shape
    tm = min(TOKEN_TILE, nt)
    vec = lambda n: _layer_spec(layer, (1, n))
    return pl.pallas_call(
        _rwkv_kernel,
        grid=(nb, nt // tm),
        in_specs=[_tok_spec(tm, RWKV_W), vec(RWKV_SHIFT), vec(BW),
                  _layer_spec(layer, (RWKV_LORA, BW)), vec(BW),
                  _layer_spec(layer, (RWKV_LORA, BW)), vec(BW), vec(BW), vec(BW), vec(BW), vec(BW),
                  _const_spec(tril.shape), _const_spec(hsum.shape)],
        out_specs=_tok_spec(tm, BW),
        out_shape=jax.ShapeDtypeStruct((nb, nt, BW), F32),
        scratch_shapes=[pltpu.VMEM((8, RWKV_SHIFT), F32), pltpu.VMEM((BW, BW), F32)],
        compiler_params=_mixer_params(),
        name="rwkv7",
    )(u, p["mu"], p["w0"], p["w2"], p["a0"], p["a2"], p["kk"], p["ka"], p["rk"], p["ln_g"],
      p["ln_b"], tril, hsum)


def _hgrn_level_matrices():
    c = CHUNK
    t = np.arange(c)[:, None]
    j = np.arange(c)[None, :]
    mats = [(j <= t)]
    for lvl in range(N_LEVEL):
        blk = c >> lvl
        mid = (t // blk) * blk + blk // 2
        upper = (t % blk) >= blk // 2
        mats.append(np.where(upper, (j >= mid) & (j <= t), (j > t) & (j < mid)))
    return jnp.asarray(np.concatenate(mats, axis=0).astype(np.float32), dtype=BF16)


def _hgrn_kernel(u_ref, lb_ref, g_ref, lvl_ref, hsum_ref, o_ref, s_ref):
    tm = u_ref.shape[0]

    @pl.when(pl.program_id(1) == 0)
    def _():
        s_ref[...] = jnp.zeros_like(s_ref)

    u = u_ref[...]
    q = u[:, 0:BW]
    kx = (1.0 - lb_ref[...]) * _sigmoid(-u[:, BW:2 * BW])
    log_f = jnp.log1p(-kx)
    v = u[:, 2 * BW:3 * BW]
    gate = u[:, 3 * BW:]

    bd01 = hsum_ref[...]
    row = lax.broadcasted_iota(jnp.int32, (CHUNK, BW), 0)
    pos = lax.broadcasted_iota(jnp.int32, (CHUNK, BW), 1) & (HEAD - 1)
    masks = []
    for lvl in range(N_LEVEL):
        sh = N_LEVEL - lvl
        half = 1 << (sh - 1)
        same = (row >> sh) == (pos >> sh)
        masks.append(same & ((row & half) != 0) & ((pos & half) == 0))
    diag = row == pos
    same_head = ((lax.broadcasted_iota(jnp.int32, (BW, BW), 0) // HEAD)
                 == (lax.broadcasted_iota(jnp.int32, (BW, BW), 1) // HEAD))

    s = s_ref[...]
    ys = []
    for j in range(0, tm // CHUNK, 2):
        pair = []
        for jj in (j, j + 1):
            sl = slice(jj * CHUNK, (jj + 1) * CHUNK)
            qj, kj, vj = q[sl], kx[sl], v[sl]
            d_all = _dot01_left(lvl_ref[...], log_f[sl])
            b = d_all[:CHUNK]
            b_last = b[CHUNK - 1:CHUNK, :]
            sc = jnp.where(diag, _dot_nt(qj, _block(kj, bd01)), 0.0)
            for lvl in range(N_LEVEL):
                e = jnp.exp(d_all[(lvl + 1) * CHUNK:(lvl + 2) * CHUNK])
                sc = jnp.where(masks[lvl], _dot_nt(qj * e, _block(kj * e, bd01)), sc)
            col = jnp.exp(jnp.broadcast_to(b_last, (2 * CHUNK, BW)).T)
            pair.append((sc, qj * jnp.exp(b), vj, kj * jnp.exp(b_last - b),
                         jnp.concatenate([col, col], axis=1)))
        k_eT = jnp.concatenate([pair[0][3], pair[1][3]], axis=0).T
        for i, (sc, q_in, vj, _, e_col) in enumerate(pair):
            ys.append(_dot(jnp.concatenate([sc, q_in], axis=1),
                           jnp.concatenate([_block(vj, bd01), s.astype(BF16)], axis=0)))
            kv = _dot(k_eT[:, i * CHUNK:(i + 1) * CHUNK], vj)
            s = e_col * s + jnp.where(same_head, kv, 0.0)
    s_ref[...] = s
    y = jnp.concatenate(ys, axis=0)
    ms = _dot01_right(y * y, bd01) * (1.0 / HEAD)
    o_ref[...] = y * lax.rsqrt(ms + EPS) * g_ref[...] * _silu(gate)


def _hgrn(layer, u, lb_all, norm_g, lvl, hsum):
    nb, nt, _ = u.shape
    tm = min(TOKEN_TILE, nt)
    return pl.pallas_call(
        _hgrn_kernel,
        grid=(nb, nt // tm),
        in_specs=[_tok_spec(tm, HGRN_W), _layer_spec(layer, (1, BW)), _layer_spec(layer, (1, BW)),
                  _const_spec(lvl.shape), _const_spec(hsum.shape)],
        out_specs=_tok_spec(tm, BW),
        out_shape=jax.ShapeDtypeStruct((nb, nt, BW), F32),
        scratch_shapes=[pltpu.VMEM((BW, BW), F32)],
        compiler_params=_mixer_params(),
        name="hgrn2",
    )(u, lb_all, norm_g, lvl, hsum)


def _ret_tables(nt, tm):
    dk = RET_QKHEAD
    pos = jnp.arange(nt, dtype=F32)
    inv_freq = 1.0 / (ROPE_BASE ** jnp.linspace(0.0, 1.0, dk // 2, dtype=F32))
    ang = pos[:, None] * inv_freq[None, :]
    cos, sin = jnp.cos(ang), jnp.sin(ang)
    cos_t = jnp.tile(jnp.concatenate([cos, cos], -1), (1, N_HEAD))
    sin_t = jnp.tile(jnp.concatenate([-sin, sin], -1), (1, N_HEAD))
    log_gamma = jnp.log1p(-jnp.exp2(-5.0 - jnp.arange(N_HEAD, dtype=F32)))
    idx = jnp.arange(tm, dtype=F32)
    rel = idx[:, None] - idx[None, :]
    decay = jnp.where(rel >= 0, jnp.exp(log_gamma[:, None, None] * jnp.maximum(rel, 0.0)), 0.0)
    zeta = jnp.exp(log_gamma[:, None] * (tm - 1 - idx))
    xi = jnp.exp(log_gamma[:, None] * (idx + 1.0))
    gamma_c = jnp.exp(log_gamma * tm)
    lanes = lambda t: jnp.repeat(t.T, dk, axis=1)
    return cos_t, sin_t, decay, lanes(zeta), lanes(xi), lanes(gamma_c[:, None])


def _ret_kernel(u_ref, cos_ref, sin_ref, dec_ref, zeta_ref, xi_ref, gam_ref, g_ref, hsum_ref,
                o_ref, s_ref):
    @pl.when(pl.program_id(1) == 0)
    def _():
        s_ref[...] = jnp.zeros_like(s_ref)

    u = u_ref[...]
    cos, sin = cos_ref[...], sin_ref[...]
    lane = lax.broadcasted_iota(jnp.int32, cos.shape, 1)
    first_half = (lane & (RET_QKHEAD // 2)) == 0

    def rope(t):
        half = RET_QKHEAD // 2
        swapped = jnp.where(first_half, pltpu.roll(t, RET_QK - half, axis=1),
                            pltpu.roll(t, half, axis=1))
        return t * cos + swapped * sin

    q = rope(u[:, 0:RET_QK])
    k = rope(u[:, RET_QK:2 * RET_QK]) * (RET_QKHEAD ** -0.5)
    v = u[:, 2 * RET_QK:2 * RET_QK + BW]
    gate = u[:, 2 * RET_QK + BW:]
    q_x = q * xi_ref[...]
    k_z = k * zeta_ref[...]
    gam = gam_ref[...]
    outs = []
    for h in range(N_HEAD):
        qs = slice(h * RET_QKHEAD, (h + 1) * RET_QKHEAD)
        hs = slice(h * HEAD, (h + 1) * HEAD)
        sc = _dot_nt(q[:, qs], k[:, qs]) * dec_ref[h]
        s = s_ref[h]
        outs.append(_dot(sc, v[:, hs]) + _dot_nt(q_x[:, qs], s))
        s_ref[h] = s * gam[:, qs] + _dot_tn(v[:, hs], k_z[:, qs])
    y = jnp.concatenate(outs, axis=1)
    ms = _dot01_right(y * y, hsum_ref[...]) * (1.0 / HEAD)
    o_ref[...] = y * lax.rsqrt(ms + EPS) * g_ref[...] * _silu(gate)


def _retention(layer, u, norm_g, tables, hsum):
    nb, nt, _ = u.shape
    tm = min(TOKEN_TILE, nt)
    cos_t, sin_t, decay, zeta, xi, gam = tables
    pos_spec = pl.BlockSpec((tm, RET_QK), lambda b, t: (t, 0))
    return pl.pallas_call(
        _ret_kernel,
        grid=(nb, nt // tm),
        in_specs=[_tok_spec(tm, RET_W), pos_spec, pos_spec, _const_spec(decay.shape),
                  _const_spec(zeta.shape), _const_spec(xi.shape), _const_spec(gam.shape),
                  _layer_spec(layer, (1, BW)), _const_spec(hsum.shape)],
        out_specs=_tok_spec(tm, BW),
        out_shape=jax.ShapeDtypeStruct((nb, nt, BW), F32),
        scratch_shapes=[pltpu.VMEM((N_HEAD, HEAD, RET_QKHEAD), F32)],
        compiler_params=_mixer_params(),
        name="retention",
    )(u, cos_t, sin_t, decay, zeta, xi, gam, norm_g, hsum)


def _lru_kernel(u_ref, cw_ref, cb_ref, wa_ref, ba_ref, wx_ref, bx_ref, lam_ref, o_ref,
                tail_ref, h_ref):
    tm = u_ref.shape[0]
    first = pl.program_id(1) == 0

    @pl.when(first)
    def _():
        tail_ref[...] = jnp.zeros_like(tail_ref)
        h_ref[...] = jnp.zeros_like(h_ref)

    u = u_ref[...]
    x = u[:, :BW]
    gate = u[:, BW:]
    tail = tail_ref[...]
    cw = cw_ref[...]
    y = cb_ref[...] + cw[CONV_WIDTH - 1:CONV_WIDTH, :] * x
    for k in range(1, CONV_WIDTH):
        y = y + cw[CONV_WIDTH - 1 - k:CONV_WIDTH - k, :] * _shift_rows(x, k, tail)
    tail_ref[...] = x[tm - 8:, :]

    r = _sigmoid(_dot(y, wa_ref[...]) + ba_ref[...])
    i = _sigmoid(_dot(y, wx_ref[...]) + bx_ref[...])
    log_a = -LRU_C * r * _softplus(-lam_ref[...])
    a = jnp.exp(log_a)
    mult = jnp.sqrt(jnp.tanh(-log_a) * (a * a + 1.0))
    row = lax.broadcasted_iota(jnp.int32, x.shape, 0)
    mult = jnp.where(row + pl.program_id(1) * tm == 0, 1.0, mult)
    b = mult * (i * y)

    d = 1
    while d < tm:
        a_sh = jnp.where(row < d, 1.0, pltpu.roll(a, d, axis=0))
        b_sh = jnp.where(row < d, 0.0, pltpu.roll(b, d, axis=0))
        b = a * b_sh + b
        a = a * a_sh
        d *= 2
    h = a * h_ref[0:1, :] + b
    h_ref[0:1, :] = h[tm - 1:tm, :]
    o_ref[...] = h * _silu(gate)


def _rglru(layer, u, p):
    nb, nt, _ = u.shape
    tm = min(TOKEN_TILE, nt)
    vec = lambda: _layer_spec(layer, (1, BW))
    mat = lambda: _layer_spec(layer, (BW, BW))
    return pl.pallas_call(
        _lru_kernel,
        grid=(nb, nt // tm),
        in_specs=[_tok_spec(tm, LRU_W), _layer_spec(layer, (CONV_WIDTH, BW)), vec(), mat(), vec(),
                  mat(), vec(), vec()],
        out_specs=_tok_spec(tm, BW),
        out_shape=jax.ShapeDtypeStruct((nb, nt, BW), F32),
        scratch_shapes=[pltpu.VMEM((8, BW), F32), pltpu.VMEM((8, BW), F32)],
        compiler_params=_mixer_params(),
        name="rglru",
    )(u, p["conv_w"], p["conv_b"], p["wa"], p["ba"], p["wx"], p["bx"], p["lam"])


def _merge_kernel(h_ref, mod_ref, g_ref, ya_ref, yb_ref, yc_ref, yd_ref, wg_ref, wb_ref, wo_ref,
                  fg_ref, o_ref, *, final_norm):
    x = h_ref[...]
    d = x.shape[-1]
    mod = mod_ref[...]
    hn = _adaln(x, g_ref[...], mod).astype(BF16)
    merged = None
    for bi, y_ref in enumerate((ya_ref, yb_ref, yc_ref, yd_ref)):
        gates = _sigmoid(jnp.dot(hn, wg_ref[:, bi * d:(bi + 1) * d], preferred_element_type=F32))
        term = gates * _dot(y_ref[...], wb_ref[bi])
        merged = term if merged is None else merged + term
    out = x + mod[2:3, :] * _dot(merged, wo_ref[...])
    if final_norm:
        out = out * lax.rsqrt(jnp.mean(out * out, axis=-1, keepdims=True) + EPS) * fg_ref[...]
    o_ref[...] = out


def _merge(layer, h, mod, norm_g, ys, w_gate, w_branch, w_out, final_g, final_norm):
    nb, nt, d = h.shape
    tm = min(TOKEN_TILE, nt)
    return pl.pallas_call(
        functools.partial(_merge_kernel, final_norm=final_norm),
        grid=(nb, nt // tm),
        in_specs=[_tok_spec(tm, d),
                  pl.BlockSpec((None, None, 3, d), lambda b, t: (layer, b, 0, 0)),
                  _layer_spec(layer, (1, d))]
                 + [_tok_spec(tm, BW)] * N_BRANCH
                 + [_layer_spec(layer, (d, N_BRANCH * d)), _layer_spec(layer, (N_BRANCH, BW, d)),
                    _layer_spec(layer, (d, d)), _const_spec((1, d))],
        out_specs=_tok_spec(tm, d),
        out_shape=jax.ShapeDtypeStruct((nb, nt, d), F32),
        compiler_params=pltpu.CompilerParams(dimension_semantics=("parallel", "parallel"),
                                             vmem_limit_bytes=VMEM_LIMIT),
        name="merge",
    )(h, mod, norm_g, *ys, w_gate, w_branch, w_out, final_g)


def _block_diag(w):
    depth, g, n, _ = w.shape
    eye = jnp.eye(g, dtype=w.dtype)
    return (w[:, :, :, None, :] * eye[None, :, None, :, None]).reshape(depth, g * n, g * n)


def kernel(x, c, norm_g, w_mod, b_mod, w_in, rwkv_mu, rwkv_w0, rwkv_w2, rwkv_a0, rwkv_a2, rwkv_kk, rwkv_ka, rwkv_rk, rwkv_ln_g, rwkv_ln_b, hgrn_lb, hgrn_norm_g, ret_norm_g, lru_conv_w, lru_conv_b, lru_wa, lru_ba, lru_wx, lru_bx, lru_lam, w_branch, w_out, final_g):
    nb, nt, d = x.shape
    depth = w_in.shape[0]
    tm = min(TOKEN_TILE, nt)
    assert nt % tm == 0 and tm % CHUNK == 0 and w_in.shape[2] == MIX_W + N_BRANCH * d

    row3 = lambda p: p.reshape(depth, 1, -1)
    w_mix = w_in[:, :, :MIX_W].astype(BF16)
    w_gate = w_in[:, :, MIX_W:].astype(BF16)
    w_branch_b = w_branch.astype(BF16)
    w_out_b = w_out.astype(BF16)
    rwkv_p = dict(mu=row3(rwkv_mu), w0=row3(rwkv_w0), w2=rwkv_w2.astype(BF16), a0=row3(rwkv_a0),
                  a2=rwkv_a2.astype(BF16), kk=row3(rwkv_kk), ka=row3(rwkv_ka), rk=row3(rwkv_rk),
                  ln_g=row3(rwkv_ln_g), ln_b=row3(rwkv_ln_b))
    lru_p = dict(conv_w=lru_conv_w, conv_b=row3(lru_conv_b), wa=_block_diag(lru_wa).astype(BF16),
                 ba=row3(lru_ba), wx=_block_diag(lru_wx).astype(BF16), bx=row3(lru_bx),
                 lam=row3(lru_lam))
    lb_p = jax.nn.softmax(hgrn_lb.astype(F32), axis=0)
    lb_all = row3(jnp.cumsum(lb_p, axis=0) - lb_p[0])

    head_id = np.arange(BW) // HEAD
    hsum = jnp.asarray((head_id[:, None] == head_id[None, :]).astype(np.float32), dtype=BF16)
    tok = np.arange(tm)
    tril = jnp.asarray(((tok[:, None] // CHUNK == tok[None, :] // CHUNK)
                        & (tok[None, :] <= tok[:, None])).astype(np.float32), dtype=BF16)
    lvl = _hgrn_level_matrices()
    ret_tab = _ret_tables(nt, tm)

    mod = _modulation(c, w_mod, b_mod).reshape(depth, nb, 3, d)
    g3 = row3(norm_g)
    h = x
    for l in range(depth):
        ua, ub, uc, ud = _inproj(l, h, mod, g3, w_mix)
        ya = _rwkv(l, ua, rwkv_p, tril, hsum)
        yb = _hgrn(l, ub, lb_all, row3(hgrn_norm_g), lvl, hsum)
        yc = _retention(l, uc, row3(ret_norm_g), ret_tab, hsum)
        yd = _rglru(l, ud, lru_p)
        h = _merge(l, h, mod, g3, (ya, yb, yc, yd), w_gate, w_branch_b, w_out_b,
                   final_g.reshape(1, d), final_norm=(l == depth - 1))
    return h
```

```python
import functools

import numpy as np
import jax
import jax.numpy as jnp
from jax import lax
from jax.experimental import pallas as pl
from jax.experimental.pallas import tpu as pltpu

F32 = jnp.float32
BF16 = jnp.bfloat16

N_BRANCH = 4
BW = 256
N_HEAD = 4
HEAD = BW // N_HEAD
LANE = 128
EPS = 1e-6
RWKV_LORA = 64
RWKV_SHIFT = 3 * BW + 2 * RWKV_LORA
RWKV_W = RWKV_SHIFT + BW
RWKV_LN_EPS = 64e-5
HGRN_W = 4 * BW
RET_QKHEAD = HEAD // 2
RET_QK = N_HEAD * RET_QKHEAD
RET_W = 2 * RET_QK + 2 * BW
ROPE_BASE = 10000.0
LRU_W = 2 * BW
CONV_WIDTH = 4
LRU_C = 8.0
MIX_W = RWKV_W + HGRN_W + RET_W + LRU_W

CHUNK = 64
N_LEVEL = 6
TOKEN_TILE = 256
VMEM_LIMIT = 48 * 1024 * 1024


def _dot(a, b):
    return jnp.dot(a.astype(BF16), b.astype(BF16), preferred_element_type=F32)


def _dot_nt(a, b):
    return lax.dot_general(a.astype(BF16), b.astype(BF16), (((1,), (1,)), ((), ())),
                           preferred_element_type=F32)


def _dot_tn(a, b):
    return lax.dot_general(a.astype(BF16), b.astype(BF16), (((0,), (0,)), ((), ())),
                           preferred_element_type=F32)


def _split2(x):
    hi = x.astype(BF16)
    lo = (x - hi.astype(F32)).astype(BF16)
    return hi, lo


def _split3(x):
    x1 = x.astype(BF16)
    r1 = x - x1.astype(F32)
    x2 = r1.astype(BF16)
    x3 = (r1 - x2.astype(F32)).astype(BF16)
    return x1, x2, x3


def _dot_hi(a, b):
    ah, al = _split2(a)
    bh, bl = _split2(b)
    d = functools.partial(jnp.dot, preferred_element_type=F32)
    return d(ah, bh) + (d(ah, bl) + d(al, bh))


def _dot01_left(m01, x):
    d = functools.partial(jnp.dot, preferred_element_type=F32)
    x1, x2, x3 = _split3(x)
    return d(m01, x1) + (d(m01, x2) + d(m01, x3))


def _dot01_right(x, m01):
    return jnp.dot(x.astype(BF16), m01, preferred_element_type=F32)


def _sigmoid(x):
    return jax.nn.sigmoid(x)


def _silu(x):
    return x * jax.nn.sigmoid(x)


def _softplus(x):
    return jnp.maximum(x, 0.0) + jnp.log1p(jnp.exp(-jnp.abs(x)))


def _adaln(x, g, mod):
    ms = jnp.mean(x * x, axis=-1, keepdims=True)
    y = x * lax.rsqrt(ms + EPS) * g
    return y * (1.0 + mod[1:2, :]) + mod[0:1, :]


def _shift_rows(x, k, tail):
    xr = pltpu.roll(x, k, axis=0)
    row = lax.broadcasted_iota(jnp.int32, (8, x.shape[1]), 0)
    head = jnp.where(row < k, pltpu.roll(tail, k, axis=0), xr[0:8, :])
    return jnp.concatenate([head, xr[8:, :]], axis=0)


def _const_spec(shape):
    nd = len(shape)
    return pl.BlockSpec(shape, lambda *_: (0,) * nd)


def _layer_spec(layer, shape):
    nd = len(shape)
    return pl.BlockSpec((None,) + tuple(shape), lambda *_: (layer,) + (0,) * nd)


def _tok_spec(tm, width):
    return pl.BlockSpec((None, tm, width), lambda b, t: (b, t, 0))


def _seq_per_step(nb):
    return 2 if nb % 2 == 0 else 1


def _seq_spec(ns, tm, width):
    return pl.BlockSpec((ns, tm, width), lambda b, t: (b, t, 0))


def _mixer_params():
    return pltpu.CompilerParams(dimension_semantics=("parallel", "arbitrary"),
                                vmem_limit_bytes=VMEM_LIMIT)


def _mod_kernel(c_ref, w_ref, b_ref, o_ref):
    o_ref[...] = _dot_hi(_silu(c_ref[...]), w_ref[...]) + b_ref[...]


def _modulation(c, w_mod, b_mod):
    depth, d, d3 = w_mod.shape
    nb = c.shape[0]
    n_col = d3 // d
    return pl.pallas_call(
        _mod_kernel,
        grid=(depth, n_col),
        in_specs=[pl.BlockSpec((nb, d), lambda l, j: (0, 0)),
                  pl.BlockSpec((None, d, d), lambda l, j: (l, 0, j)),
                  pl.BlockSpec((None, 1, d), lambda l, j: (l, 0, j))],
        out_specs=pl.BlockSpec((None, nb, d), lambda l, j: (l, 0, j)),
        out_shape=jax.ShapeDtypeStruct((depth, nb, d3), F32),
        name="modulation",
    )(c, w_mod, b_mod.reshape(depth, 1, d3))


def _inproj_kernel(x_ref, mod_ref, g_ref, w_ref, oa_ref, ob_ref, oc_ref, od_ref):
    hn = _adaln(x_ref[...], g_ref[...], mod_ref[...]).astype(BF16)
    w = w_ref
    o = 0
    for ref in (oa_ref, ob_ref, oc_ref, od_ref):
        n = ref.shape[-1]
        ref[...] = jnp.dot(hn, w[:, o:o + n], preferred_element_type=F32)
        o += n


def _inproj(layer, h, mod, norm_g, w_mix):
    nb, nt, d = h.shape
    tm = min(TOKEN_TILE, nt)
    widths = (RWKV_W, HGRN_W, RET_W, LRU_W)
    return pl.pallas_call(
        _inproj_kernel,
        grid=(nb, nt // tm),
        in_specs=[_tok_spec(tm, d),
                  pl.BlockSpec((None, None, 3, d), lambda b, t: (layer, b, 0, 0)),
                  _layer_spec(layer, (1, d)),
                  _layer_spec(layer, (d, MIX_W))],
        out_specs=[_tok_spec(tm, n) for n in widths],
        out_shape=[jax.ShapeDtypeStruct((nb, nt, n), F32) for n in widths],
        compiler_params=pltpu.CompilerParams(dimension_semantics=("parallel", "parallel"),
                                             vmem_limit_bytes=VMEM_LIMIT),
        name="inproj",
    )(h, mod, norm_g, w_mix)


def _head_lane_masks(bd01):
    return [bd01[i * HEAD:i * HEAD + CHUNK, 0:LANE] for i in range(LANE // HEAD)]


def _block(x, masks):
    xb = x.astype(BF16)
    zero = jnp.zeros((x.shape[0], LANE), BF16)
    per = LANE // HEAD
    rows = []
    for h in range(N_HEAD):
        g = h // per
        part = xb[:, g * LANE:(g + 1) * LANE] * masks[h % per]
        rows.append(jnp.concatenate([part if i == g else zero for i in range(BW // LANE)], axis=1))
    return jnp.concatenate(rows, axis=0)


def _interleave(gens):
    gens = list(gens)
    while gens:
        alive = []
        for g in gens:
            try:
                next(g)
                alive.append(g)
            except StopIteration:
                pass
        gens = alive


def _unit_lower_inverses(a_list, eye_wide, hmask):
    n = len(a_list)
    xs = [_dot(a, _block(a, hmask)) for a in a_list]
    invs = [eye_wide + a for a in a_list]
    for lvl in range(1, N_LEVEL):
        yield
        for i in range(n):
            xb = _block(xs[i], hmask)
            if lvl < N_LEVEL - 1:
                res = _dot(jnp.concatenate([xs[i], invs[i]], axis=0), xb)
                xs[i] = res[:CHUNK]
                invs[i] = invs[i] + res[CHUNK:]
            else:
                invs[i] = invs[i] + _dot(invs[i], xb)
    return invs


def _rwkv_kernel(u_ref, mu_ref, w0_ref, w2_ref, a0_ref, a2_ref, kk_ref, ka_ref, rk_ref,
                 lng_ref, lnb_ref, tril_ref, hsum_ref, o_ref, tail_ref, s_ref):
    @pl.when(pl.program_id(1) == 0)
    def _():
        tail_ref[...] = jnp.zeros_like(tail_ref)
        s_ref[...] = jnp.zeros_like(s_ref)

    params = (mu_ref, w0_ref, w2_ref, a0_ref, a2_ref, kk_ref, ka_ref, rk_ref, lng_ref, lnb_ref,
              tril_ref, hsum_ref)
    _interleave(_rwkv_seq(u_ref.at[i], *params, o_ref.at[i], tail_ref.at[i], s_ref.at[i])
                for i in range(u_ref.shape[0]))


def _rwkv_seq(u_ref, mu_ref, w0_ref, w2_ref, a0_ref, a2_ref, kk_ref, ka_ref, rk_ref,
              lng_ref, lnb_ref, tril_ref, hsum_ref, o_ref, tail_ref, s_ref):
    tm = u_ref.shape[0]
    u = u_ref[...]
    feats = u[:, :RWKV_SHIFT]
    gate = u[:, RWKV_SHIFT:]
    prev = _shift_rows(feats, 1, tail_ref[...])
    tail_ref[...] = feats[tm - 8:, :]
    x = feats + mu_ref[...] * (prev - feats)
    r = x[:, 0:BW]
    k = x[:, BW:2 * BW]
    v = x[:, 2 * BW:3 * BW]
    w_lo = x[:, 3 * BW:3 * BW + RWKV_LORA]
    a_lo = x[:, 3 * BW + RWKV_LORA:]

    hsum = hsum_ref[...]
    log_w = -float(np.exp(-0.5)) * _sigmoid(w0_ref[...] + _dot(jnp.tanh(w_lo), w2_ref[...]))
    a = _sigmoid(a0_ref[...] + _dot(a_lo, a2_ref[...]))
    kk = k * kk_ref[...]
    k = k * (1.0 + (a - 1.0) * ka_ref[...])
    kk = kk / jnp.maximum(jnp.sqrt(_dot01_right(kk * kk, hsum)), 1e-12)
    beta = kk * a

    c = _dot01_left(tril_ref[...], log_w)
    n_chunk = tm // CHUNK
    hmask = _head_lane_masks(hsum)
    row = lax.broadcasted_iota(jnp.int32, (CHUNK, BW), 0)
    pos = lax.broadcasted_iota(jnp.int32, (CHUNK, BW), 1) & (HEAD - 1)
    strict = row > pos
    lower = row >= pos
    eye_wide = (row == pos).astype(F32)
    same_head = ((lax.broadcasted_iota(jnp.int32, (BW, BW), 0) // HEAD)
                 == (lax.broadcasted_iota(jnp.int32, (BW, BW), 1) // HEAD))

    al_t, r_t, a_ab, a_ak, a_rb, a_rk, b_eT, kv_c, e_col, v_blk = ([] for _ in range(10))
    yield
    for j in range(n_chunk):
        sl = slice(j * CHUNK, (j + 1) * CHUNK)
        cj = c[sl]
        c_last = cj[CHUNK - 1:CHUNK, :]
        e_in = jnp.exp(cj)
        e_prev = jnp.exp(cj - log_w[sl])
        e_out = jnp.exp(-cj)
        e_rest = jnp.exp(c_last) * e_out
        al_t.append(-kk[sl] * e_prev)
        r_t.append(r[sl] * e_in)
        rhs = jnp.concatenate([_block(beta[sl] * e_out, hmask), _block(k[sl] * e_out, hmask)], axis=0)
        p = _dot_nt(jnp.concatenate([al_t[j], r_t[j]], axis=0), rhs)
        a_ab.append(jnp.where(strict, p[:CHUNK, :BW], 0.0))
        a_ak.append(jnp.where(strict, p[:CHUNK, BW:], 0.0))
        a_rb.append(jnp.where(lower, p[CHUNK:, :BW], 0.0))
        a_rk.append(jnp.where(lower, p[CHUNK:, BW:], 0.0))
        ends = jnp.concatenate([beta[sl] * e_rest, k[sl] * e_rest], axis=0).T
        b_eT.append(ends[:, :CHUNK])
        v_blk.append(_block(v[sl], hmask))
        kv_c.append(jnp.where(same_head, _dot(ends[:, CHUNK:], v[sl]), 0.0))
        col = jnp.exp(jnp.broadcast_to(c_last, (2 * CHUNK, BW)).T)
        e_col.append(jnp.concatenate([col, col], axis=1))
        yield
    avs = [_dot(jnp.concatenate([a_ak[j], a_rk[j]], axis=0), v_blk[j]) for j in range(n_chunk)]
    invs = yield from _unit_lower_inverses(a_ab, eye_wide, hmask)
    yield
    tws = [_dot(invs[j], jnp.concatenate([_block(al_t[j], hmask), _block(avs[j][:CHUNK], hmask)],
                                         axis=1)) for j in range(n_chunk)]
    t_al = [tw[:, :BW] for tw in tws]
    t_akv = [tw[:, BW:] for tw in tws]
    yield
    qzs = [_dot(a_rb[j], jnp.concatenate([_block(t_al[j], hmask), _block(t_akv[j], hmask)], axis=1))
           for j in range(n_chunk)]
    q_c = [r_t[j] + qzs[j][:, :BW] for j in range(n_chunk)]
    z_c = [qzs[j][:, BW:] + avs[j][CHUNK:] for j in range(n_chunk)]
    yield

    s = s_ref[...]
    ys = []
    for j in range(n_chunk):
        uo = _dot(jnp.concatenate([t_al[j], q_c[j]], axis=0), s)
        ys.append(uo[CHUNK:] + z_c[j])
        uu = uo[:CHUNK] + t_akv[j]
        yield
        s = e_col[j] * s + jnp.where(same_head, _dot(b_eT[j], uu), 0.0) + kv_c[j]
        yield
    s_ref[...] = s
    y = jnp.concatenate(ys, axis=0)

    mean = _dot01_right(y, hsum) * (1.0 / HEAD)
    yc = y - mean
    var = _dot01_right(yc * yc, hsum) * (1.0 / HEAD)
    y = yc * lax.rsqrt(var + RWKV_LN_EPS) * lng_ref[...] + lnb_ref[...]
    bonus = _dot01_right(r * k * rk_ref[...], hsum) * v
    o_ref[...] = (y + bonus) * _silu(gate)


def _rwkv(layer, u, p, tril, hsum):
    nb, nt, _ = u.shape
    tm = min(TOKEN_TILE, nt)
    vec = lambda n: _layer_spec(layer, (1, n))
    ns = _seq_per_step(nb)
    return pl.pallas_call(
        _rwkv_kernel,
        grid=(nb // ns, nt // tm),
        in_specs=[_seq_spec(ns, tm, RWKV_W), vec(RWKV_SHIFT), vec(BW),
                  _layer_spec(layer, (RWKV_LORA, BW)), vec(BW),
                  _layer_spec(layer, (RWKV_LORA, BW)), vec(BW), vec(BW), vec(BW), vec(BW), vec(BW),
                  _const_spec(tril.shape), _const_spec(hsum.shape)],
        out_specs=_seq_spec(ns, tm, BW),
        out_shape=jax.ShapeDtypeStruct((nb, nt, BW), F32),
        scratch_shapes=[pltpu.VMEM((ns, 8, RWKV_SHIFT), F32), pltpu.VMEM((ns, BW, BW), F32)],
        compiler_params=_mixer_params(),
        name="rwkv7",
    )(u, p["mu"], p["w0"], p["w2"], p["a0"], p["a2"], p["kk"], p["ka"], p["rk"], p["ln_g"],
      p["ln_b"], tril, hsum)


def _midpoint_rows(b, lvl, row):
    blk = CHUNK >> lvl
    half = blk // 2
    if blk >= 8:
        return jnp.concatenate(
            [jnp.broadcast_to(b[s + half - 1:s + half, :], (blk, b.shape[1]))
             for s in range(0, CHUNK, blk)], axis=0)
    p = row & (blk - 1)
    out = b
    for d in range(-(half - 1), half + 1):
        if d != 0:
            out = jnp.where(p == half - 1 + d, pltpu.roll(b, d % CHUNK, axis=0), out)
    return out


def _hgrn_kernel(u_ref, lb_ref, g_ref, tril_ref, hsum_ref, o_ref, s_ref):
    @pl.when(pl.program_id(1) == 0)
    def _():
        s_ref[...] = jnp.zeros_like(s_ref)

    _interleave(_hgrn_seq(u_ref.at[i], lb_ref, g_ref, tril_ref, hsum_ref, o_ref.at[i], s_ref.at[i])
                for i in range(u_ref.shape[0]))


def _hgrn_seq(u_ref, lb_ref, g_ref, tril_ref, hsum_ref, o_ref, s_ref):
    tm = u_ref.shape[0]
    u = u_ref[...]
    q = u[:, 0:BW]
    kx = (1.0 - lb_ref[...]) * _sigmoid(-u[:, BW:2 * BW])
    log_f = jnp.log1p(-kx)
    v = u[:, 2 * BW:3 * BW]
    gate = u[:, 3 * BW:]
    b_all = _dot01_left(tril_ref[...], log_f)

    hmask = _head_lane_masks(hsum_ref[...])
    row = lax.broadcasted_iota(jnp.int32, (CHUNK, BW), 0)
    pos = lax.broadcasted_iota(jnp.int32, (CHUNK, BW), 1) & (HEAD - 1)
    masks = []
    for lvl in range(N_LEVEL):
        sh = N_LEVEL - lvl
        half = 1 << (sh - 1)
        same = (row >> sh) == (pos >> sh)
        masks.append(same & ((row & half) != 0) & ((pos & half) == 0))
    diag = row == pos
    same_head = ((lax.broadcasted_iota(jnp.int32, (BW, BW), 0) // HEAD)
                 == (lax.broadcasted_iota(jnp.int32, (BW, BW), 1) // HEAD))

    yield
    parts = []
    for j in range(0, tm // CHUNK, 2):
        pair = []
        for jj in (j, j + 1):
            sl = slice(jj * CHUNK, (jj + 1) * CHUNK)
            qj, kj, vj, b = q[sl], kx[sl], v[sl], b_all[sl]
            b_last = b[CHUNK - 1:CHUNK, :]
            sc = jnp.where(diag, _dot_nt(qj, _block(kj, hmask)), 0.0)
            for lvl in range(N_LEVEL):
                e = jnp.exp(-jnp.abs(b - _midpoint_rows(b, lvl, row)))
                sc = jnp.where(masks[lvl], _dot_nt(qj * e, _block(kj * e, hmask)), sc)
            col = jnp.exp(jnp.broadcast_to(b_last, (2 * CHUNK, BW)).T)
            pair.append((sc, qj * jnp.exp(b), vj, kj * jnp.exp(b_last - b),
                         jnp.concatenate([col, col], axis=1)))
            yield
        k_eT = jnp.concatenate([pair[0][3], pair[1][3]], axis=0).T
        for i, (sc, q_in, vj, _, e_col) in enumerate(pair):
            kv = jnp.where(same_head, _dot(k_eT[:, i * CHUNK:(i + 1) * CHUNK], vj), 0.0)
            parts.append((jnp.concatenate([sc, q_in], axis=1), _block(vj, hmask), e_col, kv))
        yield
    s = s_ref[...]
    ys = []
    for lhs, v_blk, e_col, kv in parts:
        ys.append(_dot(lhs, jnp.concatenate([v_blk, s.astype(BF16)], axis=0)))
        s = e_col * s + kv
        yield
    s_ref[...] = s
    y = jnp.concatenate(ys, axis=0)
    ms = _dot01_right(y * y, hsum_ref[...]) * (1.0 / HEAD)
    o_ref[...] = y * lax.rsqrt(ms + EPS) * g_ref[...] * _silu(gate)


def _hgrn(layer, u, lb_all, norm_g, tril, hsum):
    nb, nt, _ = u.shape
    tm = min(TOKEN_TILE, nt)
    ns = _seq_per_step(nb)
    return pl.pallas_call(
        _hgrn_kernel,
        grid=(nb // ns, nt // tm),
        in_specs=[_seq_spec(ns, tm, HGRN_W), _layer_spec(layer, (1, BW)), _layer_spec(layer, (1, BW)),
                  _const_spec(tril.shape), _const_spec(hsum.shape)],
        out_specs=_seq_spec(ns, tm, BW),
        out_shape=jax.ShapeDtypeStruct((nb, nt, BW), F32),
        scratch_shapes=[pltpu.VMEM((ns, BW, BW), F32)],
        compiler_params=_mixer_params(),
        name="hgrn2",
    )(u, lb_all, norm_g, tril, hsum)


def _ret_tables(nt, tm):
    dk = RET_QKHEAD
    pos = jnp.arange(nt, dtype=F32)
    inv_freq = 1.0 / (ROPE_BASE ** jnp.linspace(0.0, 1.0, dk // 2, dtype=F32))
    ang = pos[:, None] * inv_freq[None, :]
    cos, sin = jnp.cos(ang), jnp.sin(ang)
    cos_t = jnp.tile(jnp.concatenate([cos, cos], -1), (1, N_HEAD))
    sin_t = jnp.tile(jnp.concatenate([-sin, sin], -1), (1, N_HEAD))
    log_gamma = jnp.log1p(-jnp.exp2(-5.0 - jnp.arange(N_HEAD, dtype=F32)))
    idx = jnp.arange(tm, dtype=F32)
    rel = idx[:, None] - idx[None, :]
    decay = jnp.where(rel >= 0, jnp.exp(log_gamma[:, None, None] * jnp.maximum(rel, 0.0)), 0.0)
    zeta = jnp.exp(log_gamma[:, None] * (tm - 1 - idx))
    xi = jnp.exp(log_gamma[:, None] * (idx + 1.0))
    gamma_c = jnp.exp(log_gamma * tm)
    lanes = lambda t: jnp.repeat(t.T, dk, axis=1)
    return cos_t, sin_t, decay, lanes(zeta), lanes(xi), lanes(gamma_c[:, None])


def _ret_kernel(u_ref, cos_ref, sin_ref, dec_ref, zeta_ref, xi_ref, gam_ref, g_ref, hsum_ref,
                o_ref, s_ref):
    @pl.when(pl.program_id(1) == 0)
    def _():
        s_ref[...] = jnp.zeros_like(s_ref)

    u = u_ref[...]
    cos, sin = cos_ref[...], sin_ref[...]
    lane = lax.broadcasted_iota(jnp.int32, cos.shape, 1)
    first_half = (lane & (RET_QKHEAD // 2)) == 0

    def rope(t):
        half = RET_QKHEAD // 2
        swapped = jnp.where(first_half, pltpu.roll(t, RET_QK - half, axis=1),
                            pltpu.roll(t, half, axis=1))
        return t * cos + swapped * sin

    q = rope(u[:, 0:RET_QK])
    k = rope(u[:, RET_QK:2 * RET_QK]) * (RET_QKHEAD ** -0.5)
    v = u[:, 2 * RET_QK:2 * RET_QK + BW]
    gate = u[:, 2 * RET_QK + BW:]
    q_x = q * xi_ref[...]
    k_z = k * zeta_ref[...]
    gam = gam_ref[...]
    outs = []
    for h in range(N_HEAD):
        qs = slice(h * RET_QKHEAD, (h + 1) * RET_QKHEAD)
        hs = slice(h * HEAD, (h + 1) * HEAD)
        sc = _dot_nt(q[:, qs], k[:, qs]) * dec_ref[h]
        s = s_ref[h]
        outs.append(_dot(sc, v[:, hs]) + _dot_nt(q_x[:, qs], s))
        s_ref[h] = s * gam[:, qs] + _dot_tn(v[:, hs], k_z[:, qs])
    y = jnp.concatenate(outs, axis=1)
    ms = _dot01_right(y * y, hsum_ref[...]) * (1.0 / HEAD)
    o_ref[...] = y * lax.rsqrt(ms + EPS) * g_ref[...] * _silu(gate)


def _retention(layer, u, norm_g, tables, hsum):
    nb, nt, _ = u.shape
    tm = min(TOKEN_TILE, nt)
    cos_t, sin_t, decay, zeta, xi, gam = tables
    pos_spec = pl.BlockSpec((tm, RET_QK), lambda b, t: (t, 0))
    return pl.pallas_call(
        _ret_kernel,
        grid=(nb, nt // tm),
        in_specs=[_tok_spec(tm, RET_W), pos_spec, pos_spec, _const_spec(decay.shape),
                  _const_spec(zeta.shape), _const_spec(xi.shape), _const_spec(gam.shape),
                  _layer_spec(layer, (1, BW)), _const_spec(hsum.shape)],
        out_specs=_tok_spec(tm, BW),
        out_shape=jax.ShapeDtypeStruct((nb, nt, BW), F32),
        scratch_shapes=[pltpu.VMEM((N_HEAD, HEAD, RET_QKHEAD), F32)],
        compiler_params=_mixer_params(),
        name="retention",
    )(u, cos_t, sin_t, decay, zeta, xi, gam, norm_g, hsum)


def _lru_kernel(u_ref, cw_ref, cb_ref, wa_ref, ba_ref, wx_ref, bx_ref, lam_ref, o_ref,
                tail_ref, h_ref):
    tm = u_ref.shape[0]
    first = pl.program_id(1) == 0

    @pl.when(first)
    def _():
        tail_ref[...] = jnp.zeros_like(tail_ref)
        h_ref[...] = jnp.zeros_like(h_ref)

    u = u_ref[...]
    x = u[:, :BW]
    gate = u[:, BW:]
    tail = tail_ref[...]
    cw = cw_ref[...]
    y = cb_ref[...] + cw[CONV_WIDTH - 1:CONV_WIDTH, :] * x
    for k in range(1, CONV_WIDTH):
        y = y + cw[CONV_WIDTH - 1 - k:CONV_WIDTH - k, :] * _shift_rows(x, k, tail)
    tail_ref[...] = x[tm - 8:, :]

    r = _sigmoid(_dot(y, wa_ref[...]) + ba_ref[...])
    i = _sigmoid(_dot(y, wx_ref[...]) + bx_ref[...])
    log_a = -LRU_C * r * _softplus(-lam_ref[...])
    a = jnp.exp(log_a)
    mult = jnp.sqrt(jnp.tanh(-log_a) * (a * a + 1.0))
    row = lax.broadcasted_iota(jnp.int32, x.shape, 0)
    mult = jnp.where(row + pl.program_id(1) * tm == 0, 1.0, mult)
    b = mult * (i * y)

    d = 1
    while d < tm:
        a_sh = jnp.where(row < d, 1.0, pltpu.roll(a, d, axis=0))
        b_sh = jnp.where(row < d, 0.0, pltpu.roll(b, d, axis=0))
        b = a * b_sh + b
        a = a * a_sh
        d *= 2
    h = a * h_ref[0:1, :] + b
    h_ref[0:1, :] = h[tm - 1:tm, :]
    o_ref[...] = h * _silu(gate)


def _rglru(layer, u, p):
    nb, nt, _ = u.shape
    tm = min(TOKEN_TILE, nt)
    vec = lambda: _layer_spec(layer, (1, BW))
    mat = lambda: _layer_spec(layer, (BW, BW))
    return pl.pallas_call(
        _lru_kernel,
        grid=(nb, nt // tm),
        in_specs=[_tok_spec(tm, LRU_W), _layer_spec(layer, (CONV_WIDTH, BW)), vec(), mat(), vec(),
                  mat(), vec(), vec()],
        out_specs=_tok_spec(tm, BW),
        out_shape=jax.ShapeDtypeStruct((nb, nt, BW), F32),
        scratch_shapes=[pltpu.VMEM((8, BW), F32), pltpu.VMEM((8, BW), F32)],
        compiler_params=_mixer_params(),
        name="rglru",
    )(u, p["conv_w"], p["conv_b"], p["wa"], p["ba"], p["wx"], p["bx"], p["lam"])


def _merge_kernel(h_ref, mod_ref, g_ref, ya_ref, yb_ref, yc_ref, yd_ref, wg_ref, wb_ref, wo_ref,
                  fg_ref, o_ref, *, final_norm):
    x = h_ref[...]
    d = x.shape[-1]
    mod = mod_ref[...]
    hn = _adaln(x, g_ref[...], mod).astype(BF16)
    merged = None
    for bi, y_ref in enumerate((ya_ref, yb_ref, yc_ref, yd_ref)):
        gates = _sigmoid(jnp.dot(hn, wg_ref[:, bi * d:(bi + 1) * d], preferred_element_type=F32))
        term = gates * _dot(y_ref[...], wb_ref[bi])
        merged = term if merged is None else merged + term
    out = x + mod[2:3, :] * _dot(merged, wo_ref[...])
    if final_norm:
        out = out * lax.rsqrt(jnp.mean(out * out, axis=-1, keepdims=True) + EPS) * fg_ref[...]
    o_ref[...] = out


def _merge(layer, h, mod, norm_g, ys, w_gate, w_branch, w_out, final_g, final_norm):
    nb, nt, d = h.shape
    tm = min(TOKEN_TILE, nt)
    return pl.pallas_call(
        functools.partial(_merge_kernel, final_norm=final_norm),
        grid=(nb, nt // tm),
        in_specs=[_tok_spec(tm, d),
                  pl.BlockSpec((None, None, 3, d), lambda b, t: (layer, b, 0, 0)),
                  _layer_spec(layer, (1, d))]
                 + [_tok_spec(tm, BW)] * N_BRANCH
                 + [_layer_spec(layer, (d, N_BRANCH * d)), _layer_spec(layer, (N_BRANCH, BW, d)),
                    _layer_spec(layer, (d, d)), _const_spec((1, d))],
        out_specs=_tok_spec(tm, d),
        out_shape=jax.ShapeDtypeStruct((nb, nt, d), F32),
        compiler_params=pltpu.CompilerParams(dimension_semantics=("parallel", "parallel"),
                                             vmem_limit_bytes=VMEM_LIMIT),
        name="merge",
    )(h, mod, norm_g, *ys, w_gate, w_branch, w_out, final_g)


def _block_diag(w):
    depth, g, n, _ = w.shape
    eye = jnp.eye(g, dtype=w.dtype)
    return (w[:, :, :, None, :] * eye[None, :, None, :, None]).reshape(depth, g * n, g * n)


def kernel(x, c, norm_g, w_mod, b_mod, w_in, rwkv_mu, rwkv_w0, rwkv_w2, rwkv_a0, rwkv_a2, rwkv_kk, rwkv_ka, rwkv_rk, rwkv_ln_g, rwkv_ln_b, hgrn_lb, hgrn_norm_g, ret_norm_g, lru_conv_w, lru_conv_b, lru_wa, lru_ba, lru_wx, lru_bx, lru_lam, w_branch, w_out, final_g):
    nb, nt, d = x.shape
    depth = w_in.shape[0]
    tm = min(TOKEN_TILE, nt)
    assert nt % tm == 0 and tm % CHUNK == 0 and w_in.shape[2] == MIX_W + N_BRANCH * d

    row3 = lambda p: p.reshape(depth, 1, -1)
    w_mix = w_in[:, :, :MIX_W].astype(BF16)
    w_gate = w_in[:, :, MIX_W:].astype(BF16)
    w_branch_b = w_branch.astype(BF16)
    w_out_b = w_out.astype(BF16)
    rwkv_p = dict(mu=row3(rwkv_mu), w0=row3(rwkv_w0), w2=rwkv_w2.astype(BF16), a0=row3(rwkv_a0),
                  a2=rwkv_a2.astype(BF16), kk=row3(rwkv_kk), ka=row3(rwkv_ka), rk=row3(rwkv_rk),
                  ln_g=row3(rwkv_ln_g), ln_b=row3(rwkv_ln_b))
    lru_p = dict(conv_w=lru_conv_w, conv_b=row3(lru_conv_b), wa=_block_diag(lru_wa).astype(BF16),
                 ba=row3(lru_ba), wx=_block_diag(lru_wx).astype(BF16), bx=row3(lru_bx),
                 lam=row3(lru_lam))
    lb_p = jax.nn.softmax(hgrn_lb.astype(F32), axis=0)
    lb_all = row3(jnp.cumsum(lb_p, axis=0) - lb_p[0])

    head_id = np.arange(BW) // HEAD
    hsum = jnp.asarray((head_id[:, None] == head_id[None, :]).astype(np.float32), dtype=BF16)
    tok = np.arange(tm)
    tril = jnp.asarray(((tok[:, None] // CHUNK == tok[None, :] // CHUNK)
                        & (tok[None, :] <= tok[:, None])).astype(np.float32), dtype=BF16)
    ret_tab = _ret_tables(nt, tm)

    mod = _modulation(c, w_mod, b_mod).reshape(depth, nb, 3, d)
    g3 = row3(norm_g)
    h = x
    for l in range(depth):
        ua, ub, uc, ud = _inproj(l, h, mod, g3, w_mix)
        ya = _rwkv(l, ua, rwkv_p, tril, hsum)
        yb = _hgrn(l, ub, lb_all, row3(hgrn_norm_g), tril, hsum)
        yc = _retention(l, uc, row3(ret_norm_g), ret_tab, hsum)
        yd = _rglru(l, ud, lru_p)
        h = _merge(l, h, mod, g3, (ya, yb, yc, yd), w_gate, w_branch_b, w_out_b,
                   final_g.reshape(1, d), final_norm=(l == depth - 1))
    return h
```

```python
import functools

import numpy as np
import jax
import jax.numpy as jnp
from jax import lax
from jax.experimental import pallas as pl
from jax.experimental.pallas import tpu as pltpu

F32 = jnp.float32
BF16 = jnp.bfloat16

N_BRANCH = 4
BW = 256
N_HEAD = 4
HEAD = BW // N_HEAD
LANE = 128
EPS = 1e-6
RWKV_LORA = 64
RWKV_SHIFT = 3 * BW + 2 * RWKV_LORA
RWKV_W = RWKV_SHIFT + BW
RWKV_LN_EPS = 64e-5
HGRN_W = 4 * BW
RET_QKHEAD = HEAD // 2
RET_QK = N_HEAD * RET_QKHEAD
RET_W = 2 * RET_QK + 2 * BW
ROPE_BASE = 10000.0
LRU_W = 2 * BW
CONV_WIDTH = 4
LRU_C = 8.0
MIX_W = RWKV_W + HGRN_W + RET_W + LRU_W

CHUNK = 64
N_LEVEL = 6
TOKEN_TILE = 256
VMEM_LIMIT = 56 * 1024 * 1024


def _dot(a, b):
    return jnp.dot(a.astype(BF16), b.astype(BF16), preferred_element_type=F32)


def _dot_nt(a, b):
    return lax.dot_general(a.astype(BF16), b.astype(BF16), (((1,), (1,)), ((), ())),
                           preferred_element_type=F32)


def _dot_tn(a, b):
    return lax.dot_general(a.astype(BF16), b.astype(BF16), (((0,), (0,)), ((), ())),
                           preferred_element_type=F32)


def _split2(x):
    hi = x.astype(BF16)
    lo = (x - hi.astype(F32)).astype(BF16)
    return hi, lo


def _split3(x):
    x1 = x.astype(BF16)
    r1 = x - x1.astype(F32)
    x2 = r1.astype(BF16)
    x3 = (r1 - x2.astype(F32)).astype(BF16)
    return x1, x2, x3


def _dot_hi(a, b):
    ah, al = _split2(a)
    bh, bl = _split2(b)
    d = functools.partial(jnp.dot, preferred_element_type=F32)
    return d(ah, bh) + (d(ah, bl) + d(al, bh))


def _dot01_left(m01, x):
    d = functools.partial(jnp.dot, preferred_element_type=F32)
    x1, x2, x3 = _split3(x)
    return d(m01, x1) + (d(m01, x2) + d(m01, x3))


def _head_sum(x, m01):
    return jnp.dot(x.astype(BF16), m01, preferred_element_type=F32)


def _sigmoid(x):
    return jax.nn.sigmoid(x)


def _silu(x):
    return x * jax.nn.sigmoid(x)


def _softplus(x):
    return jnp.maximum(x, 0.0) + jnp.log1p(jnp.exp(-jnp.abs(x)))


def _adaln(x, g, mod):
    ms = jnp.mean(x * x, axis=-1, keepdims=True)
    y = x * lax.rsqrt(ms + EPS) * g
    return y * (1.0 + mod[1:2, :]) + mod[0:1, :]


def _shift_rows(x, k, tail):
    xr = pltpu.roll(x, k, axis=0)
    row = lax.broadcasted_iota(jnp.int32, (8, x.shape[1]), 0)
    head = jnp.where(row < k, pltpu.roll(tail, k, axis=0), xr[0:8, :])
    return jnp.concatenate([head, xr[8:, :]], axis=0)


def _round_robin(gens, stagger=0):
    gens = list(gens)
    results = [None] * len(gens)
    alive = list(range(len(gens)))
    rounds = 0
    while alive:
        still = []
        for idx in alive:
            if rounds < idx * stagger:
                still.append(idx)
                continue
            try:
                next(gens[idx])
                still.append(idx)
            except StopIteration as stop:
                results[idx] = stop.value
        alive = still
        rounds += 1
        yield
    return results


def _run(gen):
    for _ in gen:
        pass


def _mod_kernel(c_ref, w_ref, b_ref, o_ref):
    o_ref[...] = _dot_hi(_silu(c_ref[...]), w_ref[...]) + b_ref[...]


def _modulation(c, w_mod, b_mod):
    depth, d, d3 = w_mod.shape
    nb = c.shape[0]
    n_col = d3 // d
    return pl.pallas_call(
        _mod_kernel,
        grid=(depth, n_col),
        in_specs=[pl.BlockSpec((nb, d), lambda l, j: (0, 0)),
                  pl.BlockSpec((None, d, d), lambda l, j: (l, 0, j)),
                  pl.BlockSpec((None, 1, d), lambda l, j: (l, 0, j))],
        out_specs=pl.BlockSpec((None, nb, d), lambda l, j: (l, 0, j)),
        out_shape=jax.ShapeDtypeStruct((depth, nb, d3), F32),
        name="modulation",
    )(c, w_mod, b_mod.reshape(depth, 1, d3))


def _head_lane_masks(bd01):
    return [bd01[i * HEAD:i * HEAD + CHUNK, 0:LANE] for i in range(LANE // HEAD)]


def _block(x, masks):
    xb = x.astype(BF16)
    zero = jnp.zeros((x.shape[0], LANE), BF16)
    per = LANE // HEAD
    rows = []
    for h in range(N_HEAD):
        g = h // per
        part = xb[:, g * LANE:(g + 1) * LANE] * masks[h % per]
        rows.append(jnp.concatenate([part if i == g else zero for i in range(BW // LANE)], axis=1))
    return jnp.concatenate(rows, axis=0)


def _unit_lower_inverses(a_list, eye_wide, hmask):
    n = len(a_list)
    xs = [_dot(a, _block(a, hmask)) for a in a_list]
    invs = [eye_wide + a for a in a_list]
    for lvl in range(1, N_LEVEL):
        yield
        for i in range(n):
            xb = _block(xs[i], hmask)
            if lvl < N_LEVEL - 1:
                res = _dot(jnp.concatenate([xs[i], invs[i]], axis=0), xb)
                xs[i] = res[:CHUNK]
                invs[i] = invs[i] + res[CHUNK:]
            else:
                invs[i] = invs[i] + _dot(invs[i], xb)
    return invs


def _rwkv_seq(u, p, tril_ref, hsum_ref, tail_ref, s_ref):
    tm = u.shape[0]
    feats = u[:, :RWKV_SHIFT]
    gate = u[:, RWKV_SHIFT:]
    prev = _shift_rows(feats, 1, tail_ref[...])
    tail_ref[...] = feats[tm - 8:, :]
    x = feats + p["mu"][...] * (prev - feats)
    r = x[:, 0:BW]
    k = x[:, BW:2 * BW]
    v = x[:, 2 * BW:3 * BW]
    w_lo = x[:, 3 * BW:3 * BW + RWKV_LORA]
    a_lo = x[:, 3 * BW + RWKV_LORA:]

    hsum = hsum_ref[...]
    log_w = -float(np.exp(-0.5)) * _sigmoid(p["w0"][...] + _dot(jnp.tanh(w_lo), p["w2"][...]))
    a = _sigmoid(p["a0"][...] + _dot(a_lo, p["a2"][...]))
    kk = k * p["kk"][...]
    k = k * (1.0 + (a - 1.0) * p["ka"][...])
    kk = kk / jnp.maximum(jnp.sqrt(_head_sum(kk * kk, hsum)), 1e-12)
    beta = kk * a

    c = _dot01_left(tril_ref[...], log_w)
    n_chunk = tm // CHUNK
    hmask = _head_lane_masks(hsum)
    row = lax.broadcasted_iota(jnp.int32, (CHUNK, BW), 0)
    pos = lax.broadcasted_iota(jnp.int32, (CHUNK, BW), 1) & (HEAD - 1)
    strict = row > pos
    lower = row >= pos
    eye_wide = (row == pos).astype(F32)
    same_head = ((lax.broadcasted_iota(jnp.int32, (BW, BW), 0) // HEAD)
                 == (lax.broadcasted_iota(jnp.int32, (BW, BW), 1) // HEAD))

    al_t, r_t, a_ab, a_ak, a_rb, a_rk, b_eT, kv_c, e_col, v_blk = ([] for _ in range(10))
    yield
    for j in range(n_chunk):
        sl = slice(j * CHUNK, (j + 1) * CHUNK)
        cj = c[sl]
        c_last = cj[CHUNK - 1:CHUNK, :]
        e_in = jnp.exp(cj)
        e_prev = jnp.exp(cj - log_w[sl])
        e_out = jnp.exp(-cj)
        e_rest = jnp.exp(c_last) * e_out
        al_t.append(-kk[sl] * e_prev)
        r_t.append(r[sl] * e_in)
        rhs = jnp.concatenate([_block(beta[sl] * e_out, hmask), _block(k[sl] * e_out, hmask)], axis=0)
        pm = _dot_nt(jnp.concatenate([al_t[j], r_t[j]], axis=0), rhs)
        a_ab.append(jnp.where(strict, pm[:CHUNK, :BW], 0.0))
        a_ak.append(jnp.where(strict, pm[:CHUNK, BW:], 0.0))
        a_rb.append(jnp.where(lower, pm[CHUNK:, :BW], 0.0))
        a_rk.append(jnp.where(lower, pm[CHUNK:, BW:], 0.0))
        ends = jnp.concatenate([beta[sl] * e_rest, k[sl] * e_rest], axis=0).T
        b_eT.append(ends[:, :CHUNK])
        v_blk.append(_block(v[sl], hmask))
        kv_c.append(jnp.where(same_head, _dot(ends[:, CHUNK:], v[sl]), 0.0))
        col = jnp.exp(jnp.broadcast_to(c_last, (2 * CHUNK, BW)).T)
        e_col.append(jnp.concatenate([col, col], axis=1))
        yield
    avs = [_dot(jnp.concatenate([a_ak[j], a_rk[j]], axis=0), v_blk[j]) for j in range(n_chunk)]
    invs = yield from _unit_lower_inverses(a_ab, eye_wide, hmask)
    yield
    tws = [_dot(invs[j], jnp.concatenate([_block(al_t[j], hmask), _block(avs[j][:CHUNK], hmask)],
                                         axis=1)) for j in range(n_chunk)]
    t_al = [tw[:, :BW] for tw in tws]
    t_akv = [tw[:, BW:] for tw in tws]
    yield
    qzs = [_dot(a_rb[j], jnp.concatenate([_block(t_al[j], hmask), _block(t_akv[j], hmask)], axis=1))
           for j in range(n_chunk)]
    q_c = [r_t[j] + qzs[j][:, :BW] for j in range(n_chunk)]
    z_c = [qzs[j][:, BW:] + avs[j][CHUNK:] for j in range(n_chunk)]
    yield

    s = s_ref[...]
    ys = []
    for j in range(n_chunk):
        uo = _dot(jnp.concatenate([t_al[j], q_c[j]], axis=0), s)
        ys.append(uo[CHUNK:] + z_c[j])
        uu = uo[:CHUNK] + t_akv[j]
        yield
        s = e_col[j] * s + jnp.where(same_head, _dot(b_eT[j], uu), 0.0) + kv_c[j]
        yield
    s_ref[...] = s
    y = jnp.concatenate(ys, axis=0)

    mean = _head_sum(y, hsum) * (1.0 / HEAD)
    yc = y - mean
    var = _head_sum(yc * yc, hsum) * (1.0 / HEAD)
    y = yc * lax.rsqrt(var + RWKV_LN_EPS) * p["ln_g"][...] + p["ln_b"][...]
    bonus = _head_sum(r * k * p["rk"][...], hsum) * v
    return (y + bonus) * _silu(gate)


def _midpoint_rows(b, lvl, row):
    blk = CHUNK >> lvl
    half = blk // 2
    if blk >= 8:
        return jnp.concatenate(
            [jnp.broadcast_to(b[s + half - 1:s + half, :], (blk, b.shape[1]))
             for s in range(0, CHUNK, blk)], axis=0)
    p = row & (blk - 1)
    out = b
    for d in range(-(half - 1), half + 1):
        if d != 0:
            out = jnp.where(p == half - 1 + d, pltpu.roll(b, d % CHUNK, axis=0), out)
    return out


def _hgrn_seq(u, lb_ref, g_ref, tril_ref, hsum_ref, s_ref):
    tm = u.shape[0]
    q = u[:, 0:BW]
    kx = (1.0 - lb_ref[...]) * _sigmoid(-u[:, BW:2 * BW])
    log_f = jnp.log1p(-kx)
    v = u[:, 2 * BW:3 * BW]
    gate = u[:, 3 * BW:]
    b_all = _dot01_left(tril_ref[...], log_f)

    hmask = _head_lane_masks(hsum_ref[...])
    row = lax.broadcasted_iota(jnp.int32, (CHUNK, BW), 0)
    pos = lax.broadcasted_iota(jnp.int32, (CHUNK, BW), 1) & (HEAD - 1)
    masks = []
    for lvl in range(N_LEVEL):
        sh = N_LEVEL - lvl
        half = 1 << (sh - 1)
        same = (row >> sh) == (pos >> sh)
        masks.append(same & ((row & half) != 0) & ((pos & half) == 0))
    diag = row == pos
    same_head = ((lax.broadcasted_iota(jnp.int32, (BW, BW), 0) // HEAD)
                 == (lax.broadcasted_iota(jnp.int32, (BW, BW), 1) // HEAD))

    yield
    parts = []
    for j in range(0, tm // CHUNK, 2):
        pair = []
        for jj in (j, j + 1):
            sl = slice(jj * CHUNK, (jj + 1) * CHUNK)
            qj, kj, vj, b = q[sl], kx[sl], v[sl], b_all[sl]
            b_last = b[CHUNK - 1:CHUNK, :]
            sc = jnp.where(diag, _dot_nt(qj, _block(kj, hmask)), 0.0)
            for lvl in range(N_LEVEL):
                e = jnp.exp(-jnp.abs(b - _midpoint_rows(b, lvl, row)))
                sc = jnp.where(masks[lvl], _dot_nt(qj * e, _block(kj * e, hmask)), sc)
            col = jnp.exp(jnp.broadcast_to(b_last, (2 * CHUNK, BW)).T)
            pair.append((sc, qj * jnp.exp(b), vj, kj * jnp.exp(b_last - b),
                         jnp.concatenate([col, col], axis=1)))
            yield
        k_eT = jnp.concatenate([pair[0][3], pair[1][3]], axis=0).T
        for i, (sc, q_in, vj, _, e_col) in enumerate(pair):
            kv = jnp.where(same_head, _dot(k_eT[:, i * CHUNK:(i + 1) * CHUNK], vj), 0.0)
            parts.append((jnp.concatenate([sc, q_in], axis=1), _block(vj, hmask), e_col, kv))
        yield
    s = s_ref[...]
    ys = []
    for lhs, v_blk, e_col, kv in parts:
        ys.append(_dot(lhs, jnp.concatenate([v_blk, s.astype(BF16)], axis=0)))
        s = e_col * s + kv
        yield
    s_ref[...] = s
    y = jnp.concatenate(ys, axis=0)
    ms = _head_sum(y * y, hsum_ref[...]) * (1.0 / HEAD)
    return y * lax.rsqrt(ms + EPS) * g_ref[...] * _silu(gate)


def _ret_tables(nt, tm):
    dk = RET_QKHEAD
    pos = jnp.arange(nt, dtype=F32)
    inv_freq = 1.0 / (ROPE_BASE ** jnp.linspace(0.0, 1.0, dk // 2, dtype=F32))
    ang = pos[:, None] * inv_freq[None, :]
    cos, sin = jnp.cos(ang), jnp.sin(ang)
    cos_t = jnp.tile(jnp.concatenate([cos, cos], -1), (1, N_HEAD))
    sin_t = jnp.tile(jnp.concatenate([-sin, sin], -1), (1, N_HEAD))
    log_gamma = jnp.log1p(-jnp.exp2(-5.0 - jnp.arange(N_HEAD, dtype=F32)))
    idx = jnp.arange(tm, dtype=F32)
    rel = idx[:, None] - idx[None, :]
    decay = jnp.where(rel >= 0, jnp.exp(log_gamma[:, None, None] * jnp.maximum(rel, 0.0)), 0.0)
    zeta = jnp.exp(log_gamma[:, None] * (tm - 1 - idx))
    xi = jnp.exp(log_gamma[:, None] * (idx + 1.0))
    gamma_c = jnp.exp(log_gamma * tm)
    lanes = lambda t: jnp.repeat(t.T, dk, axis=1)
    return cos_t, sin_t, decay, lanes(zeta), lanes(xi), lanes(gamma_c[:, None])


def _ret_seq(u, cos_ref, sin_ref, dec_ref, zeta_ref, xi_ref, gam_ref, g_ref, hsum_ref, s_ref):
    cos, sin = cos_ref[...], sin_ref[...]
    lane = lax.broadcasted_iota(jnp.int32, cos.shape, 1)
    first_half = (lane & (RET_QKHEAD // 2)) == 0

    def rope(t):
        half = RET_QKHEAD // 2
        swapped = jnp.where(first_half, pltpu.roll(t, RET_QK - half, axis=1),
                            pltpu.roll(t, half, axis=1))
        return t * cos + swapped * sin

    q = rope(u[:, 0:RET_QK])
    k = rope(u[:, RET_QK:2 * RET_QK]) * (RET_QKHEAD ** -0.5)
    v = u[:, 2 * RET_QK:2 * RET_QK + BW]
    gate = u[:, 2 * RET_QK + BW:]
    q_x = q * xi_ref[...]
    k_z = k * zeta_ref[...]
    gam = gam_ref[...]
    outs = []
    for h in range(N_HEAD):
        yield
        qs = slice(h * RET_QKHEAD, (h + 1) * RET_QKHEAD)
        hs = slice(h * HEAD, (h + 1) * HEAD)
        sc = _dot_nt(q[:, qs], k[:, qs]) * dec_ref[h]
        s = s_ref[h]
        outs.append(_dot(sc, v[:, hs]) + _dot_nt(q_x[:, qs], s))
        s_ref[h] = s * gam[:, qs] + _dot_tn(v[:, hs], k_z[:, qs])
    yield
    y = jnp.concatenate(outs, axis=1)
    ms = _head_sum(y * y, hsum_ref[...]) * (1.0 / HEAD)
    return y * lax.rsqrt(ms + EPS) * g_ref[...] * _silu(gate)


def _lru_seq(u, p, tail_ref, h_ref):
    tm = u.shape[0]
    x = u[:, :BW]
    gate = u[:, BW:]
    tail = tail_ref[...]
    cw = p["conv_w"][...]
    y = p["conv_b"][...] + cw[CONV_WIDTH - 1:CONV_WIDTH, :] * x
    for k in range(1, CONV_WIDTH):
        y = y + cw[CONV_WIDTH - 1 - k:CONV_WIDTH - k, :] * _shift_rows(x, k, tail)
    tail_ref[...] = x[tm - 8:, :]
    yield

    r = _sigmoid(_dot(y, p["wa"][...]) + p["ba"][...])
    i = _sigmoid(_dot(y, p["wx"][...]) + p["bx"][...])
    log_a = -LRU_C * r * _softplus(-p["lam"][...])
    a = jnp.exp(log_a)
    mult = jnp.sqrt(jnp.tanh(-log_a) * (a * a + 1.0))
    row = lax.broadcasted_iota(jnp.int32, x.shape, 0)
    mult = jnp.where(row + pl.program_id(1) * tm == 0, 1.0, mult)
    b = mult * (i * y)

    d = 1
    while d < tm:
        yield
        a_sh = jnp.where(row < d, 1.0, pltpu.roll(a, d, axis=0))
        b_sh = jnp.where(row < d, 0.0, pltpu.roll(b, d, axis=0))
        b = a * b_sh + b
        a = a * a_sh
        d *= 2
    h = a * h_ref[0:1, :] + b
    h_ref[0:1, :] = h[tm - 1:tm, :]
    return h * _silu(gate)


_RWKV_KEYS = ("mu", "w0", "w2", "a0", "a2", "kk", "ka", "rk", "ln_g", "ln_b")
_LRU_KEYS = ("conv_w", "conv_b", "wa", "ba", "wx", "bx", "lam")
_RET_KEYS = ("cos", "sin", "decay", "zeta", "xi", "gam")
N_SCRATCH = 6


def _layer_seq(i, x_ref, mod_ref, g_ref, w_in_ref, rwkv_p, tril_ref, hsum_ref, lb_ref, hg_ref,
               ret_t, rg_ref, lru_p, wb_ref, wo_ref, fg_ref, o_ref, scratch, final_norm):
    rw_tail, rw_s, hg_s, rt_s, lr_tail, lr_h = (ref.at[i] for ref in scratch)
    x = x_ref[i]
    d = x.shape[-1]
    mod = mod_ref[i]
    hn = _adaln(x, g_ref[...], mod).astype(BF16)
    us = []
    o = 0
    for n in (RWKV_W, HGRN_W, RET_W, LRU_W):
        yield
        us.append(jnp.dot(hn, w_in_ref[:, o:o + n], preferred_element_type=F32))
        o += n
    ys = yield from _round_robin([
        _rwkv_seq(us[0], rwkv_p, tril_ref, hsum_ref, rw_tail, rw_s),
        _hgrn_seq(us[1], lb_ref, hg_ref, tril_ref, hsum_ref, hg_s),
        _ret_seq(us[2], *ret_t, rg_ref, hsum_ref, rt_s),
        _lru_seq(us[3], lru_p, lr_tail, lr_h)])
    merged = None
    for bi in range(N_BRANCH):
        yield
        gates = _sigmoid(jnp.dot(hn, w_in_ref[:, MIX_W + bi * d:MIX_W + (bi + 1) * d],
                                 preferred_element_type=F32))
        term = gates * _dot(ys[bi], wb_ref[bi])
        merged = term if merged is None else merged + term
    yield
    out = x + mod[2:3, :] * _dot(merged, wo_ref[...])
    if final_norm:
        out = out * lax.rsqrt(jnp.mean(out * out, axis=-1, keepdims=True) + EPS) * fg_ref[...]
    o_ref[i] = out


def _layer_kernel(*refs, final_norm, stagger):
    it = iter(refs)
    take = lambda n: [next(it) for _ in range(n)]
    x_ref, mod_ref, g_ref, w_in_ref = take(4)
    rwkv_p = dict(zip(_RWKV_KEYS, take(len(_RWKV_KEYS))))
    tril_ref, hsum_ref, lb_ref, hg_ref = take(4)
    ret_t = take(len(_RET_KEYS))
    (rg_ref,) = take(1)
    lru_p = dict(zip(_LRU_KEYS, take(len(_LRU_KEYS))))
    wb_ref, wo_ref, fg_ref, o_ref = take(4)
    scratch = take(N_SCRATCH)

    @pl.when(pl.program_id(1) == 0)
    def _():
        for ref in scratch:
            ref[...] = jnp.zeros_like(ref)

    _run(_round_robin(
        [_layer_seq(i, x_ref, mod_ref, g_ref, w_in_ref, rwkv_p, tril_ref, hsum_ref, lb_ref, hg_ref,
                    ret_t, rg_ref, lru_p, wb_ref, wo_ref, fg_ref, o_ref, scratch, final_norm)
         for i in range(x_ref.shape[0])], stagger))


def _resident(shape, index_map):
    return pl.BlockSpec(shape, index_map, pipeline_mode=pl.Buffered(1))


def _layer(layer, h, mod, p, consts, final_norm):
    nb, nt, d = h.shape
    tm = min(TOKEN_TILE, nt)
    ns = 2 if nb % 2 == 0 else 1
    const = lambda a: _resident(a.shape, lambda b, t, nd=a.ndim: (0,) * nd)
    per_layer = lambda a: _resident((None,) + a.shape[1:],
                                    lambda b, t, nd=a.ndim: (layer,) + (0,) * (nd - 1))
    tok = pl.BlockSpec((ns, tm, d), lambda b, t: (b, t, 0))
    pos = pl.BlockSpec((tm, RET_QK), lambda b, t: (t, 0))
    ret = consts["ret"]
    operands = ([h, mod, p["norm_g"], p["w_in"]] + [p["rwkv"][k] for k in _RWKV_KEYS]
                + [consts["tril"], consts["hsum"], p["hgrn_lb"], p["hgrn_g"]]
                + [ret[k] for k in _RET_KEYS] + [p["ret_g"]] + [p["lru"][k] for k in _LRU_KEYS]
                + [p["w_branch"], p["w_out"], p["final_g"]])
    in_specs = ([tok, pl.BlockSpec((None, ns, 3, d), lambda b, t: (layer, b, 0, 0)),
                 per_layer(p["norm_g"]), per_layer(p["w_in"])]
                + [per_layer(p["rwkv"][k]) for k in _RWKV_KEYS]
                + [const(consts["tril"]), const(consts["hsum"]), per_layer(p["hgrn_lb"]),
                   per_layer(p["hgrn_g"]), pos, pos]
                + [const(ret[k]) for k in _RET_KEYS[2:]] + [per_layer(p["ret_g"])]
                + [per_layer(p["lru"][k]) for k in _LRU_KEYS]
                + [per_layer(p["w_branch"]), per_layer(p["w_out"]), const(p["final_g"])])
    scratch = [pltpu.VMEM((ns, 8, RWKV_SHIFT), F32), pltpu.VMEM((ns, BW, BW), F32),
               pltpu.VMEM((ns, BW, BW), F32), pltpu.VMEM((ns, N_HEAD, HEAD, RET_QKHEAD), F32),
               pltpu.VMEM((ns, 8, BW), F32), pltpu.VMEM((ns, 8, BW), F32)]
    assert len(scratch) == N_SCRATCH
    return pl.pallas_call(
        functools.partial(_layer_kernel, final_norm=final_norm, stagger=LAYER_STAGGER),
        grid=(nb // ns, nt // tm),
        in_specs=in_specs,
        out_specs=tok,
        out_shape=jax.ShapeDtypeStruct((nb, nt, d), F32),
        scratch_shapes=scratch,
        compiler_params=pltpu.CompilerParams(dimension_semantics=("parallel", "arbitrary"),
                                             vmem_limit_bytes=VMEM_LIMIT),
        name="layer",
    )(*operands)


LAYER_STAGGER = 6


def _block_diag(w):
    depth, g, n, _ = w.shape
    eye = jnp.eye(g, dtype=w.dtype)
    return (w[:, :, :, None, :] * eye[None, :, None, :, None]).reshape(depth, g * n, g * n)


def kernel(x, c, norm_g, w_mod, b_mod, w_in, rwkv_mu, rwkv_w0, rwkv_w2, rwkv_a0, rwkv_a2, rwkv_kk, rwkv_ka, rwkv_rk, rwkv_ln_g, rwkv_ln_b, hgrn_lb, hgrn_norm_g, ret_norm_g, lru_conv_w, lru_conv_b, lru_wa, lru_ba, lru_wx, lru_bx, lru_lam, w_branch, w_out, final_g):
    nb, nt, d = x.shape
    depth = w_in.shape[0]
    tm = min(TOKEN_TILE, nt)
    assert nt % tm == 0 and tm % (2 * CHUNK) == 0 and CHUNK == HEAD
    assert w_in.shape[2] == MIX_W + N_BRANCH * d

    row3 = lambda a: a.reshape(depth, 1, -1)
    lb_p = jax.nn.softmax(hgrn_lb.astype(F32), axis=0)
    params = dict(
        norm_g=row3(norm_g), w_in=w_in.astype(BF16),
        rwkv=dict(mu=row3(rwkv_mu), w0=row3(rwkv_w0), w2=rwkv_w2.astype(BF16), a0=row3(rwkv_a0),
                  a2=rwkv_a2.astype(BF16), kk=row3(rwkv_kk), ka=row3(rwkv_ka), rk=row3(rwkv_rk),
                  ln_g=row3(rwkv_ln_g), ln_b=row3(rwkv_ln_b)),
        hgrn_lb=row3(jnp.cumsum(lb_p, axis=0) - lb_p[0]), hgrn_g=row3(hgrn_norm_g),
        ret_g=row3(ret_norm_g),
        lru=dict(conv_w=lru_conv_w, conv_b=row3(lru_conv_b), wa=_block_diag(lru_wa).astype(BF16),
                 ba=row3(lru_ba), wx=_block_diag(lru_wx).astype(BF16), bx=row3(lru_bx),
                 lam=row3(lru_lam)),
        w_branch=w_branch.astype(BF16), w_out=w_out.astype(BF16), final_g=final_g.reshape(1, d))

    head_id = np.arange(BW) // HEAD
    tok = np.arange(tm)
    consts = dict(
        hsum=jnp.asarray((head_id[:, None] == head_id[None, :]).astype(np.float32), dtype=BF16),
        tril=jnp.asarray(((tok[:, None] // CHUNK == tok[None, :] // CHUNK)
                          & (tok[None, :] <= tok[:, None])).astype(np.float32), dtype=BF16),
        ret=dict(zip(_RET_KEYS, _ret_tables(nt, tm))))

    mod = _modulation(c, w_mod, b_mod).reshape(depth, nb, 3, d)
    h = x
    for l in range(depth):
        h = _layer(l, h, mod, params, consts, final_norm=(l == depth - 1))
    return h
```

```python
import functools

import numpy as np
import jax
import jax.numpy as jnp
from jax import lax
from jax.experimental import pallas as pl
from jax.experimental.pallas import tpu as pltpu

F32 = jnp.float32
BF16 = jnp.bfloat16

N_BRANCH = 4
BW = 256
N_HEAD = 4
HEAD = BW // N_HEAD
LANE = 128
EPS = 1e-6
RWKV_LORA = 64
RWKV_SHIFT = 3 * BW + 2 * RWKV_LORA
RWKV_W = RWKV_SHIFT + BW
RWKV_LN_EPS = 64e-5
HGRN_W = 4 * BW
RET_QKHEAD = HEAD // 2
RET_QK = N_HEAD * RET_QKHEAD
RET_W = 2 * RET_QK + 2 * BW
ROPE_BASE = 10000.0
LRU_W = 2 * BW
CONV_WIDTH = 4
LRU_C = 8.0
MIX_W = RWKV_W + HGRN_W + RET_W + LRU_W

CHUNK = 64
N_LEVEL = 6
TOKEN_TILE = 256
SEQ_PER_STEP = 2
GATE_PIECE = 256
LAYER_STAGGER = 10
VMEM_LIMIT = 56 * 1024 * 1024


def _dot(a, b):
    return jnp.dot(a.astype(BF16), b.astype(BF16), preferred_element_type=F32)


def _dot_nt(a, b):
    return lax.dot_general(a.astype(BF16), b.astype(BF16), (((1,), (1,)), ((), ())),
                           preferred_element_type=F32)


def _dot_tn(a, b):
    return lax.dot_general(a.astype(BF16), b.astype(BF16), (((0,), (0,)), ((), ())),
                           preferred_element_type=F32)


def _split2(x):
    hi = x.astype(BF16)
    lo = (x - hi.astype(F32)).astype(BF16)
    return hi, lo


def _dot_hi(a, b):
    ah, al = _split2(a)
    bh, bl = _split2(b)
    d = functools.partial(jnp.dot, preferred_element_type=F32)
    return d(ah, bh) + (d(ah, bl) + d(al, bh))


def _dot01_left(m01, x):
    d = functools.partial(jnp.dot, preferred_element_type=F32)
    hi, lo = _split2(x)
    return d(m01, hi) + d(m01, lo)


def _head_sum(x, m01):
    return jnp.dot(x.astype(BF16), m01, preferred_element_type=F32)


def _sigmoid(x):
    return jax.nn.sigmoid(x)


def _silu(x):
    return x * jax.nn.sigmoid(x)


def _softplus(x):
    return jnp.maximum(x, 0.0) + jnp.log1p(jnp.exp(-jnp.abs(x)))


def _adaln(x, g, mod):
    ms = jnp.mean(x * x, axis=-1, keepdims=True)
    y = x * lax.rsqrt(ms + EPS) * g
    return y * (1.0 + mod[1:2, :]) + mod[0:1, :]


def _shift_rows(x, k, tail):
    xr = pltpu.roll(x, k, axis=0)
    row = lax.broadcasted_iota(jnp.int32, (8, x.shape[1]), 0)
    head = jnp.where(row < k, pltpu.roll(tail, k, axis=0), xr[0:8, :])
    return jnp.concatenate([head, xr[8:, :]], axis=0)


def _round_robin(gens, stagger=0):
    gens = list(gens)
    results = [None] * len(gens)
    alive = list(range(len(gens)))
    rounds = 0
    while alive:
        still = []
        for idx in alive:
            if rounds < idx * stagger:
                still.append(idx)
                continue
            try:
                next(gens[idx])
                still.append(idx)
            except StopIteration as stop:
                results[idx] = stop.value
        alive = still
        rounds += 1
        yield
    return results


def _run(gen):
    for _ in gen:
        pass


def _mod_kernel(c_ref, w_ref, b_ref, o_ref):
    o_ref[...] = _dot_hi(_silu(c_ref[...]), w_ref[...]) + b_ref[...]


def _modulation(c, w_mod, b_mod):
    depth, d, d3 = w_mod.shape
    nb = c.shape[0]
    n_col = d3 // d
    return pl.pallas_call(
        _mod_kernel,
        grid=(depth, n_col),
        in_specs=[pl.BlockSpec((nb, d), lambda l, j: (0, 0)),
                  pl.BlockSpec((None, d, d), lambda l, j: (l, 0, j)),
                  pl.BlockSpec((None, 1, d), lambda l, j: (l, 0, j))],
        out_specs=pl.BlockSpec((None, nb, d), lambda l, j: (l, 0, j)),
        out_shape=jax.ShapeDtypeStruct((depth, nb, d3), F32),
        name="modulation",
    )(c, w_mod, b_mod.reshape(depth, 1, d3))


def _head_lane_masks(bd01):
    return [bd01[i * HEAD:i * HEAD + CHUNK, 0:LANE] for i in range(LANE // HEAD)]


def _block(x, masks):
    xb = x.astype(BF16)
    zero = jnp.zeros((x.shape[0], LANE), BF16)
    per = LANE // HEAD
    rows = []
    for h in range(N_HEAD):
        g = h // per
        part = xb[:, g * LANE:(g + 1) * LANE] * masks[h % per]
        rows.append(jnp.concatenate([part if i == g else zero for i in range(BW // LANE)], axis=1))
    return jnp.concatenate(rows, axis=0)


def _rows_to_wide(xt, c0):
    return jnp.concatenate([xt[h * HEAD:(h + 1) * HEAD, c0:c0 + CHUNK] for h in range(N_HEAD)],
                           axis=1)


def _unit_lower_inverses(a_list, eye_wide, hmask):
    n = len(a_list)
    xs = [_dot(a, _block(a, hmask)) for a in a_list]
    invs = [eye_wide + a for a in a_list]
    for lvl in range(1, N_LEVEL):
        yield
        for i in range(n):
            xb = _block(xs[i], hmask)
            if lvl < N_LEVEL - 1:
                res = _dot(jnp.concatenate([xs[i], invs[i]], axis=0), xb)
                xs[i] = res[:CHUNK]
                invs[i] = invs[i] + res[CHUNK:]
            else:
                invs[i] = invs[i] + _dot(invs[i], xb)
    return invs


def _rwkv_seq(u, p, tril_ref, hsum_ref, tail_ref, s_ref):
    tm = u.shape[0]
    feats = u[:, :RWKV_SHIFT]
    gate = u[:, RWKV_SHIFT:]
    prev = _shift_rows(feats, 1, tail_ref[...])
    tail_ref[...] = feats[tm - 8:, :]
    x = feats + p["mu"][...] * (prev - feats)
    r = x[:, 0:BW]
    k = x[:, BW:2 * BW]
    v = x[:, 2 * BW:3 * BW]
    w_lo = x[:, 3 * BW:3 * BW + RWKV_LORA]
    a_lo = x[:, 3 * BW + RWKV_LORA:]

    hsum = hsum_ref[...]
    log_w = -float(np.exp(-0.5)) * _sigmoid(p["w0"][...] + _dot(jnp.tanh(w_lo), p["w2"][...]))
    a = _sigmoid(p["a0"][...] + _dot(a_lo, p["a2"][...]))
    kk = k * p["kk"][...]
    k = k * (1.0 + (a - 1.0) * p["ka"][...])
    kk = kk / jnp.maximum(jnp.sqrt(_head_sum(kk * kk, hsum)), 1e-12)
    beta = kk * a

    c = _dot01_left(tril_ref[...], log_w)
    n_chunk = tm // CHUNK
    hmask = _head_lane_masks(hsum)
    row = lax.broadcasted_iota(jnp.int32, (CHUNK, BW), 0)
    pos = lax.broadcasted_iota(jnp.int32, (CHUNK, BW), 1) & (HEAD - 1)
    strict = row > pos
    lower = row >= pos
    eye_wide = (row == pos).astype(F32)

    al_t, r_t, a_ab, a_ak, a_rb, a_rk, b_eT, kv_c, e_col, v_blk = ([] for _ in range(10))
    yield
    for j in range(n_chunk):
        sl = slice(j * CHUNK, (j + 1) * CHUNK)
        cj = c[sl]
        c_last = cj[CHUNK - 1:CHUNK, :]
        e_in = jnp.exp(cj)
        e_prev = jnp.exp(cj - log_w[sl])
        e_out = jnp.exp(-cj)
        e_rest = jnp.exp(c_last) * e_out
        al_t.append(-kk[sl] * e_prev)
        r_t.append(r[sl] * e_in)
        rhs = jnp.concatenate([_block(beta[sl] * e_out, hmask), _block(k[sl] * e_out, hmask)], axis=0)
        pm = _dot_nt(jnp.concatenate([al_t[j], r_t[j]], axis=0), rhs)
        a_ab.append(jnp.where(strict, pm[:CHUNK, :BW], 0.0))
        a_ak.append(jnp.where(strict, pm[:CHUNK, BW:], 0.0))
        a_rb.append(jnp.where(lower, pm[CHUNK:, :BW], 0.0))
        a_rk.append(jnp.where(lower, pm[CHUNK:, BW:], 0.0))
        ends = jnp.concatenate([beta[sl] * e_rest, k[sl] * e_rest], axis=0).T
        b_eT.append(_rows_to_wide(ends, 0))
        v_blk.append(_block(v[sl], hmask))
        kv_c.append(_dot(_rows_to_wide(ends, CHUNK), v_blk[j]))
        col = jnp.exp(jnp.broadcast_to(c_last, (2 * CHUNK, BW)).T)
        e_col.append(_rows_to_wide(col, 0))
        yield
    avs = [_dot(jnp.concatenate([a_ak[j], a_rk[j]], axis=0), v_blk[j]) for j in range(n_chunk)]
    invs = yield from _unit_lower_inverses(a_ab, eye_wide, hmask)
    yield
    tws = [_dot(invs[j], jnp.concatenate([_block(al_t[j], hmask), _block(avs[j][:CHUNK], hmask)],
                                         axis=1)) for j in range(n_chunk)]
    t_al = [tw[:, :BW] for tw in tws]
    t_akv = [tw[:, BW:] for tw in tws]
    yield
    qzs = [_dot(a_rb[j], jnp.concatenate([_block(t_al[j], hmask), _block(t_akv[j], hmask)], axis=1))
           for j in range(n_chunk)]
    q_c = [r_t[j] + qzs[j][:, :BW] for j in range(n_chunk)]
    z_c = [qzs[j][:, BW:] + avs[j][CHUNK:] for j in range(n_chunk)]
    yield

    s = s_ref[...]
    ys = []
    for j in range(n_chunk):
        uo = _dot(jnp.concatenate([t_al[j], q_c[j]], axis=0), _block(s, hmask))
        ys.append(uo[CHUNK:] + z_c[j])
        uu = uo[:CHUNK] + t_akv[j]
        yield
        s = e_col[j] * s + _dot(b_eT[j], _block(uu, hmask)) + kv_c[j]
        yield
    s_ref[...] = s
    y = jnp.concatenate(ys, axis=0)

    mean = _head_sum(y, hsum) * (1.0 / HEAD)
    yc = y - mean
    var = _head_sum(yc * yc, hsum) * (1.0 / HEAD)
    y = yc * lax.rsqrt(var + RWKV_LN_EPS) * p["ln_g"][...] + p["ln_b"][...]
    bonus = _head_sum(r * k * p["rk"][...], hsum) * v
    return (y + bonus) * _silu(gate)


def _midpoint_rows(b, lvl, row):
    blk = CHUNK >> lvl
    half = blk // 2
    if blk >= 8:
        return jnp.concatenate(
            [jnp.broadcast_to(b[s + half - 1:s + half, :], (blk, b.shape[1]))
             for s in range(0, CHUNK, blk)], axis=0)
    p = row & (blk - 1)
    out = b
    for d in range(-(half - 1), half + 1):
        if d != 0:
            out = jnp.where(p == half - 1 + d, pltpu.roll(b, d % CHUNK, axis=0), out)
    return out


def _hgrn_seq(u, lb_ref, g_ref, tril_ref, hsum_ref, s_ref):
    tm = u.shape[0]
    q = u[:, 0:BW]
    kx = (1.0 - lb_ref[...]) * _sigmoid(-u[:, BW:2 * BW])
    log_f = jnp.log(1.0 - kx)
    v = u[:, 2 * BW:3 * BW]
    gate = u[:, 3 * BW:]
    b_all = _dot01_left(tril_ref[...], log_f)

    hmask = _head_lane_masks(hsum_ref[...])
    row = lax.broadcasted_iota(jnp.int32, (CHUNK, BW), 0)
    pos = lax.broadcasted_iota(jnp.int32, (CHUNK, BW), 1) & (HEAD - 1)
    masks = []
    for lvl in range(N_LEVEL):
        sh = N_LEVEL - lvl
        half = 1 << (sh - 1)
        same = (row >> sh) == (pos >> sh)
        masks.append(same & ((row & half) != 0) & ((pos & half) == 0))
    diag = row == pos

    yield
    parts = []
    for j in range(0, tm // CHUNK, 2):
        pair = []
        for jj in (j, j + 1):
            sl = slice(jj * CHUNK, (jj + 1) * CHUNK)
            qj, kj, vj, b = q[sl], kx[sl], v[sl], b_all[sl]
            b_last = b[CHUNK - 1:CHUNK, :]
            qb, kb = qj.astype(BF16), kj.astype(BF16)
            sc = jnp.where(diag, _dot_nt(qb, _block(kb, hmask)), 0.0)
            for lvl in range(N_LEVEL):
                e = jnp.exp(-jnp.abs(b - _midpoint_rows(b, lvl, row))).astype(BF16)
                sc = jnp.where(masks[lvl], _dot_nt(qb * e, _block(kb * e, hmask)), sc)
            col = jnp.exp(jnp.broadcast_to(b_last, (2 * CHUNK, BW)).T)
            pair.append((sc, qj * jnp.exp(b), _block(vj, hmask), kj * jnp.exp(b_last - b),
                         _rows_to_wide(col, 0)))
            yield
        k_eT = jnp.concatenate([pair[0][3], pair[1][3]], axis=0).T
        for i, (sc, q_in, v_blk, _, e_col) in enumerate(pair):
            kv = _dot(_rows_to_wide(k_eT, i * CHUNK), v_blk)
            parts.append((jnp.concatenate([sc, q_in], axis=1), v_blk, e_col, kv))
        yield
    s = s_ref[...]
    ys = []
    for lhs, v_blk, e_col, kv in parts:
        ys.append(_dot(lhs, jnp.concatenate([v_blk, _block(s, hmask)], axis=0)))
        s = e_col * s + kv
        yield
    s_ref[...] = s
    y = jnp.concatenate(ys, axis=0)
    ms = _head_sum(y * y, hsum_ref[...]) * (1.0 / HEAD)
    return y * lax.rsqrt(ms + EPS) * g_ref[...] * _silu(gate)


def _ret_tables(nt, tm):
    dk = RET_QKHEAD
    pos = jnp.arange(nt, dtype=F32)
    inv_freq = 1.0 / (ROPE_BASE ** jnp.linspace(0.0, 1.0, dk // 2, dtype=F32))
    ang = pos[:, None] * inv_freq[None, :]
    cos, sin = jnp.cos(ang), jnp.sin(ang)
    cos_t = jnp.tile(jnp.concatenate([cos, cos], -1), (1, N_HEAD))
    sin_t = jnp.tile(jnp.concatenate([-sin, sin], -1), (1, N_HEAD))
    log_gamma = jnp.log1p(-jnp.exp2(-5.0 - jnp.arange(N_HEAD, dtype=F32)))
    idx = jnp.arange(tm, dtype=F32)
    rel = idx[:, None] - idx[None, :]
    decay = jnp.where(rel >= 0, jnp.exp(log_gamma[:, None, None] * jnp.maximum(rel, 0.0)), 0.0)
    zeta = jnp.exp(log_gamma[:, None] * (tm - 1 - idx))
    xi = jnp.exp(log_gamma[:, None] * (idx + 1.0))
    gamma_c = jnp.exp(log_gamma * tm)
    lanes = lambda t: jnp.repeat(t.T, dk, axis=1)
    return cos_t, sin_t, decay, lanes(zeta), lanes(xi), lanes(gamma_c[:, None])


def _ret_seq(u, cos_ref, sin_ref, dec_ref, zeta_ref, xi_ref, gam_ref, g_ref, hsum_ref, s_ref):
    cos, sin = cos_ref[...], sin_ref[...]
    lane = lax.broadcasted_iota(jnp.int32, cos.shape, 1)
    first_half = (lane & (RET_QKHEAD // 2)) == 0

    def rope(t):
        half = RET_QKHEAD // 2
        swapped = jnp.where(first_half, pltpu.roll(t, RET_QK - half, axis=1),
                            pltpu.roll(t, half, axis=1))
        return t * cos + swapped * sin

    q = rope(u[:, 0:RET_QK])
    k = rope(u[:, RET_QK:2 * RET_QK]) * (RET_QKHEAD ** -0.5)
    v = u[:, 2 * RET_QK:2 * RET_QK + BW]
    gate = u[:, 2 * RET_QK + BW:]
    q_x = q * xi_ref[...]
    k_z = k * zeta_ref[...]
    gam = gam_ref[...]
    outs = []
    for h in range(N_HEAD):
        yield
        qs = slice(h * RET_QKHEAD, (h + 1) * RET_QKHEAD)
        hs = slice(h * HEAD, (h + 1) * HEAD)
        sc = _dot_nt(q[:, qs], k[:, qs]) * dec_ref[h]
        s = s_ref[h]
        outs.append(_dot(sc, v[:, hs]) + _dot_nt(q_x[:, qs], s))
        s_ref[h] = s * gam[:, qs] + _dot_tn(v[:, hs], k_z[:, qs])
    yield
    y = jnp.concatenate(outs, axis=1)
    ms = _head_sum(y * y, hsum_ref[...]) * (1.0 / HEAD)
    return y * lax.rsqrt(ms + EPS) * g_ref[...] * _silu(gate)


def _lru_seq(u, p, tail_ref, h_ref):
    tm = u.shape[0]
    x = u[:, :BW]
    gate = u[:, BW:]
    tail = tail_ref[...]
    cw = p["conv_w"][...]
    y = p["conv_b"][...] + cw[CONV_WIDTH - 1:CONV_WIDTH, :] * x
    for k in range(1, CONV_WIDTH):
        y = y + cw[CONV_WIDTH - 1 - k:CONV_WIDTH - k, :] * _shift_rows(x, k, tail)
    tail_ref[...] = x[tm - 8:, :]
    yield

    r = _sigmoid(_dot(y, p["wa"][...]) + p["ba"][...])
    i = _sigmoid(_dot(y, p["wx"][...]) + p["bx"][...])
    log_a = -LRU_C * r * _softplus(-p["lam"][...])
    a = jnp.exp(log_a)
    mult = jnp.sqrt(jnp.tanh(-log_a) * (a * a + 1.0))
    row = lax.broadcasted_iota(jnp.int32, x.shape, 0)
    mult = jnp.where(row + pl.program_id(1) * tm == 0, 1.0, mult)
    b = mult * (i * y)

    d = 1
    while d < tm:
        yield
        if d % 8:
            a_sh = jnp.where(row < d, 1.0, pltpu.roll(a, d, axis=0))
            b_sh = jnp.where(row < d, 0.0, pltpu.roll(b, d, axis=0))
            b = a * b_sh + b
            a = a * a_sh
        else:
            b = jnp.concatenate([b[:d], a[d:] * b[:tm - d] + b[d:]], axis=0)
            a = jnp.concatenate([a[:d], a[d:] * a[:tm - d]], axis=0)
        d *= 2
    h = a * h_ref[0:1, :] + b
    h_ref[0:1, :] = h[tm - 1:tm, :]
    return h * _silu(gate)


_RWKV_KEYS = ("mu", "w0", "w2", "a0", "a2", "kk", "ka", "rk", "ln_g", "ln_b")
_LRU_KEYS = ("conv_w", "conv_b", "wa", "ba", "wx", "bx", "lam")
_RET_KEYS = ("cos", "sin", "decay", "zeta", "xi", "gam")
N_SCRATCH = 6


def _then_project(mixer, wb_ref, branch):
    y = yield from mixer
    yield
    return _dot(y, wb_ref[branch])


def _when_ready(inputs, key, make):
    while key not in inputs:
        yield
    return (yield from make(inputs[key]))


def _feed_seq(hn, w_in_ref, d, inputs):
    o = 0
    for key, n in enumerate((RWKV_W, HGRN_W, RET_W, LRU_W)):
        pieces = []
        for c0 in range(0, n, GATE_PIECE):
            w = min(GATE_PIECE, n - c0)
            pieces.append(jnp.dot(hn, w_in_ref[:, o + c0:o + c0 + w], preferred_element_type=F32))
            yield
        inputs[key] = jnp.concatenate(pieces, axis=1)
        o += n
    gates = []
    for bi in range(N_BRANCH):
        pieces = []
        for c0 in range(0, d, GATE_PIECE):
            col = MIX_W + bi * d + c0
            pieces.append(_sigmoid(jnp.dot(hn, w_in_ref[:, col:col + GATE_PIECE],
                                           preferred_element_type=F32).astype(BF16)))
            yield
        gates.append(jnp.concatenate(pieces, axis=1))
    return gates


def _layer_seq(i, x_ref, mod_ref, g_ref, w_in_ref, rwkv_p, tril_ref, hsum_ref, lb_ref, hg_ref,
               ret_t, rg_ref, lru_p, wb_ref, wo_ref, fg_ref, o_ref, scratch, final_norm):
    rw_tail, rw_s, hg_s, rt_s, lr_tail, lr_h = (ref.at[i] for ref in scratch)
    x = x_ref[i]
    d = x.shape[-1]
    mod = mod_ref[i]
    hn = _adaln(x, g_ref[...], mod).astype(BF16)
    inputs = {}
    mixers = (lambda u: _rwkv_seq(u, rwkv_p, tril_ref, hsum_ref, rw_tail, rw_s),
              lambda u: _hgrn_seq(u, lb_ref, hg_ref, tril_ref, hsum_ref, hg_s),
              lambda u: _ret_seq(u, *ret_t, rg_ref, hsum_ref, rt_s),
              lambda u: _lru_seq(u, lru_p, lr_tail, lr_h))
    gates, *projs = yield from _round_robin(
        [_feed_seq(hn, w_in_ref, d, inputs)]
        + [_then_project(_when_ready(inputs, bi, mixers[bi]), wb_ref, bi) for bi in range(N_BRANCH)])
    merged = None
    for bi in range(N_BRANCH):
        term = gates[bi] * projs[bi]
        merged = term if merged is None else merged + term
    yield
    out = x + mod[2:3, :] * _dot(merged, wo_ref[...])
    if final_norm:
        out = out * lax.rsqrt(jnp.mean(out * out, axis=-1, keepdims=True) + EPS) * fg_ref[...]
    o_ref[i] = out


def _layer_kernel(*refs, final_norm, stagger):
    it = iter(refs)
    take = lambda n: [next(it) for _ in range(n)]
    x_ref, mod_ref, g_ref, w_in_ref = take(4)
    rwkv_p = dict(zip(_RWKV_KEYS, take(len(_RWKV_KEYS))))
    tril_ref, hsum_ref, lb_ref, hg_ref = take(4)
    ret_t = take(len(_RET_KEYS))
    (rg_ref,) = take(1)
    lru_p = dict(zip(_LRU_KEYS, take(len(_LRU_KEYS))))
    wb_ref, wo_ref, fg_ref, o_ref = take(4)
    scratch = take(N_SCRATCH)

    @pl.when(pl.program_id(1) == 0)
    def _():
        for ref in scratch:
            ref[...] = jnp.zeros_like(ref)

    _run(_round_robin(
        [_layer_seq(i, x_ref, mod_ref, g_ref, w_in_ref, rwkv_p, tril_ref, hsum_ref, lb_ref, hg_ref,
                    ret_t, rg_ref, lru_p, wb_ref, wo_ref, fg_ref, o_ref, scratch, final_norm)
         for i in range(x_ref.shape[0])], stagger))


def _resident(shape, index_map):
    return pl.BlockSpec(shape, index_map, pipeline_mode=pl.Buffered(1))


def _layer(layer, h, mod, p, consts, final_norm):
    nb, nt, d = h.shape
    tm = min(TOKEN_TILE, nt)
    ns = SEQ_PER_STEP if nb % SEQ_PER_STEP == 0 else 1
    const = lambda a: _resident(a.shape, lambda b, t, nd=a.ndim: (0,) * nd)
    per_layer = lambda a: _resident((None,) + a.shape[1:],
                                    lambda b, t, nd=a.ndim: (layer,) + (0,) * (nd - 1))
    tok = pl.BlockSpec((ns, tm, d), lambda b, t: (b, t, 0))
    pos = pl.BlockSpec((tm, RET_QK), lambda b, t: (t, 0))
    ret = consts["ret"]
    operands = ([h, mod, p["norm_g"], p["w_in"]] + [p["rwkv"][k] for k in _RWKV_KEYS]
                + [consts["tril"], consts["hsum"], p["hgrn_lb"], p["hgrn_g"]]
                + [ret[k] for k in _RET_KEYS] + [p["ret_g"]] + [p["lru"][k] for k in _LRU_KEYS]
                + [p["w_branch"], p["w_out"], p["final_g"]])
    in_specs = ([tok, pl.BlockSpec((None, ns, 3, d), lambda b, t: (layer, b, 0, 0)),
                 per_layer(p["norm_g"]), per_layer(p["w_in"])]
                + [per_layer(p["rwkv"][k]) for k in _RWKV_KEYS]
                + [const(consts["tril"]), const(consts["hsum"]), per_layer(p["hgrn_lb"]),
                   per_layer(p["hgrn_g"]), pos, pos]
                + [const(ret[k]) for k in _RET_KEYS[2:]] + [per_layer(p["ret_g"])]
                + [per_layer(p["lru"][k]) for k in _LRU_KEYS]
                + [per_layer(p["w_branch"]), per_layer(p["w_out"]), const(p["final_g"])])
    scratch = [pltpu.VMEM((ns, 8, RWKV_SHIFT), F32), pltpu.VMEM((ns, HEAD, BW), F32),
               pltpu.VMEM((ns, HEAD, BW), F32), pltpu.VMEM((ns, N_HEAD, HEAD, RET_QKHEAD), F32),
               pltpu.VMEM((ns, 8, BW), F32), pltpu.VMEM((ns, 8, BW), F32)]
    assert len(scratch) == N_SCRATCH
    return pl.pallas_call(
        functools.partial(_layer_kernel, final_norm=final_norm, stagger=LAYER_STAGGER),
        grid=(nb // ns, nt // tm),
        in_specs=in_specs,
        out_specs=tok,
        out_shape=jax.ShapeDtypeStruct((nb, nt, d), F32),
        scratch_shapes=scratch,
        compiler_params=pltpu.CompilerParams(dimension_semantics=("parallel", "arbitrary"),
                                             vmem_limit_bytes=VMEM_LIMIT),
        name="layer",
    )(*operands)


def _block_diag(w):
    depth, g, n, _ = w.shape
    eye = jnp.eye(g, dtype=w.dtype)
    return (w[:, :, :, None, :] * eye[None, :, None, :, None]).reshape(depth, g * n, g * n)


def kernel(x, c, norm_g, w_mod, b_mod, w_in, rwkv_mu, rwkv_w0, rwkv_w2, rwkv_a0, rwkv_a2, rwkv_kk, rwkv_ka, rwkv_rk, rwkv_ln_g, rwkv_ln_b, hgrn_lb, hgrn_norm_g, ret_norm_g, lru_conv_w, lru_conv_b, lru_wa, lru_ba, lru_wx, lru_bx, lru_lam, w_branch, w_out, final_g):
    nb, nt, d = x.shape
    depth = w_in.shape[0]
    tm = min(TOKEN_TILE, nt)
    assert nt % tm == 0 and tm % (2 * CHUNK) == 0 and CHUNK == HEAD
    assert w_in.shape[2] == MIX_W + N_BRANCH * d

    row3 = lambda a: a.reshape(depth, 1, -1)
    lb_p = jax.nn.softmax(hgrn_lb.astype(F32), axis=0)
    params = dict(
        norm_g=row3(norm_g), w_in=w_in.astype(BF16),
        rwkv=dict(mu=row3(rwkv_mu), w0=row3(rwkv_w0), w2=rwkv_w2.astype(BF16), a0=row3(rwkv_a0),
                  a2=rwkv_a2.astype(BF16), kk=row3(rwkv_kk), ka=row3(rwkv_ka), rk=row3(rwkv_rk),
                  ln_g=row3(rwkv_ln_g), ln_b=row3(rwkv_ln_b)),
        hgrn_lb=row3(jnp.cumsum(lb_p, axis=0) - lb_p[0]), hgrn_g=row3(hgrn_norm_g),
        ret_g=row3(ret_norm_g),
        lru=dict(conv_w=lru_conv_w, conv_b=row3(lru_conv_b), wa=_block_diag(lru_wa).astype(BF16),
                 ba=row3(lru_ba), wx=_block_diag(lru_wx).astype(BF16), bx=row3(lru_bx),
                 lam=row3(lru_lam)),
        w_branch=w_branch.astype(BF16), w_out=w_out.astype(BF16), final_g=final_g.reshape(1, d))

    head_id = np.arange(BW) // HEAD
    tok = np.arange(tm)
    consts = dict(
        hsum=jnp.asarray((head_id[:, None] == head_id[None, :]).astype(np.float32), dtype=BF16),
        tril=jnp.asarray(((tok[:, None] // CHUNK == tok[None, :] // CHUNK)
                          & (tok[None, :] <= tok[:, None])).astype(np.float32), dtype=BF16),
        ret=dict(zip(_RET_KEYS, _ret_tables(nt, tm))))

    mod = _modulation(c, w_mod, b_mod).reshape(depth, nb, 3, d)
    h = x
    for l in range(depth):
        h = _layer(l, h, mod, params, consts, final_norm=(l == depth - 1))
    return h
```

```python
import functools

import numpy as np
import jax
import jax.numpy as jnp
from jax import lax
from jax.experimental import pallas as pl
from jax.experimental.pallas import tpu as pltpu

F32 = jnp.float32
BF16 = jnp.bfloat16

N_BRANCH = 4
BW = 256
N_HEAD = 4
HEAD = BW // N_HEAD
LANE = 128
EPS = 1e-6
RWKV_LORA = 64
RWKV_SHIFT = 3 * BW + 2 * RWKV_LORA
RWKV_W = RWKV_SHIFT + BW
RWKV_LN_EPS = 64e-5
HGRN_W = 4 * BW
RET_QKHEAD = HEAD // 2
RET_QK = N_HEAD * RET_QKHEAD
RET_W = 2 * RET_QK + 2 * BW
ROPE_BASE = 10000.0
LRU_W = 2 * BW
CONV_WIDTH = 4
LRU_C = 8.0
MIX_W = RWKV_W + HGRN_W + RET_W + LRU_W

CHUNK = 64
N_LEVEL = 6
TOKEN_TILE = 256
SEQ_PER_STEP = 2
GATE_PIECE = 256
LAYER_STAGGER = 10
VMEM_LIMIT = 56 * 1024 * 1024


def _dot(a, b):
    return jnp.dot(a.astype(BF16), b.astype(BF16), preferred_element_type=F32)


def _dot_nt(a, b):
    return lax.dot_general(a.astype(BF16), b.astype(BF16), (((1,), (1,)), ((), ())),
                           preferred_element_type=F32)


def _dot_tn(a, b):
    return lax.dot_general(a.astype(BF16), b.astype(BF16), (((0,), (0,)), ((), ())),
                           preferred_element_type=F32)


def _split2(x):
    hi = x.astype(BF16)
    lo = (x - hi.astype(F32)).astype(BF16)
    return hi, lo


def _dot_hi(a, b):
    ah, al = _split2(a)
    bh, bl = _split2(b)
    d = functools.partial(jnp.dot, preferred_element_type=F32)
    return d(ah, bh) + (d(ah, bl) + d(al, bh))


def _dot01_left(m01, x):
    d = functools.partial(jnp.dot, preferred_element_type=F32)
    hi, lo = _split2(x)
    return d(m01, hi) + d(m01, lo)


def _head_sum(x, m01):
    return jnp.dot(x.astype(BF16), m01, preferred_element_type=F32)


def _sigmoid(x):
    return jax.nn.sigmoid(x)


def _silu(x):
    return x * jax.nn.sigmoid(x)


def _softplus(x):
    return jnp.maximum(x, 0.0) + jnp.log1p(jnp.exp(-jnp.abs(x)))


def _adaln(x, g, mod):
    ms = jnp.mean(x * x, axis=-1, keepdims=True)
    y = x * lax.rsqrt(ms + EPS) * g
    return y * (1.0 + mod[1:2, :]) + mod[0:1, :]


def _shift_rows(x, k, tail):
    xr = pltpu.roll(x, k, axis=0)
    row = lax.broadcasted_iota(jnp.int32, (8, x.shape[1]), 0)
    head = jnp.where(row < k, pltpu.roll(tail, k, axis=0), xr[0:8, :])
    return jnp.concatenate([head, xr[8:, :]], axis=0)


def _round_robin(gens, stagger=0):
    gens = list(gens)
    results = [None] * len(gens)
    alive = list(range(len(gens)))
    rounds = 0
    while alive:
        still = []
        for idx in alive:
            if rounds < idx * stagger:
                still.append(idx)
                continue
            try:
                next(gens[idx])
                still.append(idx)
            except StopIteration as stop:
                results[idx] = stop.value
        alive = still
        rounds += 1
        yield
    return results


def _run(gen):
    for _ in gen:
        pass


def _mod_kernel(c_ref, w_ref, b_ref, o_ref):
    o_ref[...] = _dot_hi(_silu(c_ref[...]), w_ref[...]) + b_ref[...]


def _modulation(c, w_mod, b_mod):
    depth, d, d3 = w_mod.shape
    nb = c.shape[0]
    n_col = d3 // d
    return pl.pallas_call(
        _mod_kernel,
        grid=(depth, n_col),
        in_specs=[pl.BlockSpec((nb, d), lambda l, j: (0, 0)),
                  pl.BlockSpec((None, d, d), lambda l, j: (l, 0, j)),
                  pl.BlockSpec((None, 1, d), lambda l, j: (l, 0, j))],
        out_specs=pl.BlockSpec((None, nb, d), lambda l, j: (l, 0, j)),
        out_shape=jax.ShapeDtypeStruct((depth, nb, d3), F32),
        name="modulation",
    )(c, w_mod, b_mod.reshape(depth, 1, d3))


def _head_lane_masks(bd01):
    return [bd01[i * HEAD:i * HEAD + CHUNK, 0:LANE] for i in range(LANE // HEAD)]


def _block(x, masks):
    xb = x.astype(BF16)
    zero = jnp.zeros((x.shape[0], LANE), BF16)
    per = LANE // HEAD
    rows = []
    for h in range(N_HEAD):
        g = h // per
        part = xb[:, g * LANE:(g + 1) * LANE] * masks[h % per]
        rows.append(jnp.concatenate([part if i == g else zero for i in range(BW // LANE)], axis=1))
    return jnp.concatenate(rows, axis=0)


def _rows_to_wide(xt, c0):
    return jnp.concatenate([xt[h * HEAD:(h + 1) * HEAD, c0:c0 + CHUNK] for h in range(N_HEAD)],
                           axis=1)


def _unit_lower_inverses(a_list, eye_wide, hmask):
    n = len(a_list)
    xs = [_dot(a, _block(a, hmask)) for a in a_list]
    invs = [eye_wide + a for a in a_list]
    for lvl in range(1, N_LEVEL):
        yield
        for i in range(n):
            xb = _block(xs[i], hmask)
            if lvl < N_LEVEL - 1:
                res = _dot(jnp.concatenate([xs[i], invs[i]], axis=0), xb)
                xs[i] = res[:CHUNK]
                invs[i] = invs[i] + res[CHUNK:]
            else:
                invs[i] = invs[i] + _dot(invs[i], xb)
    return invs


def _rwkv_seq(u, p, tril_ref, hsum_ref, tail_ref, s_ref):
    tm = u.shape[0]
    feats = u[:, :RWKV_SHIFT]
    gate = u[:, RWKV_SHIFT:]
    prev = _shift_rows(feats, 1, tail_ref[...])
    tail_ref[...] = feats[tm - 8:, :]
    x = feats + p["mu"][...] * (prev - feats)
    r = x[:, 0:BW]
    k = x[:, BW:2 * BW]
    v = x[:, 2 * BW:3 * BW]
    w_lo = x[:, 3 * BW:3 * BW + RWKV_LORA]
    a_lo = x[:, 3 * BW + RWKV_LORA:]

    hsum = hsum_ref[...]
    log_w = -float(np.exp(-0.5)) * _sigmoid(p["w0"][...] + _dot(jnp.tanh(w_lo), p["w2"][...]))
    a = _sigmoid(p["a0"][...] + _dot(a_lo, p["a2"][...]))
    kk = k * p["kk"][...]
    k = k * (1.0 + (a - 1.0) * p["ka"][...])
    kk = kk / jnp.maximum(jnp.sqrt(_head_sum(kk * kk, hsum)), 1e-12)
    beta = kk * a

    c = _dot01_left(tril_ref[...], log_w)
    n_chunk = tm // CHUNK
    hmask = _head_lane_masks(hsum)
    row = lax.broadcasted_iota(jnp.int32, (CHUNK, BW), 0)
    pos = lax.broadcasted_iota(jnp.int32, (CHUNK, BW), 1) & (HEAD - 1)
    strict = row > pos
    lower = row >= pos
    eye_wide = (row == pos).astype(F32)

    al_t, r_t, a_ab, a_ak, a_rb, a_rk, b_eT, kv_c, e_col, v_blk = ([] for _ in range(10))
    yield
    for j in range(n_chunk):
        sl = slice(j * CHUNK, (j + 1) * CHUNK)
        cj = c[sl]
        c_last = cj[CHUNK - 1:CHUNK, :]
        e_in = jnp.exp(cj)
        e_prev = jnp.exp(cj - log_w[sl])
        e_out = jnp.exp(-cj)
        e_rest = jnp.exp(c_last) * e_out
        al_t.append(-kk[sl] * e_prev)
        r_t.append(r[sl] * e_in)
        rhs = jnp.concatenate([_block(beta[sl] * e_out, hmask), _block(k[sl] * e_out, hmask)], axis=0)
        pm = _dot_nt(jnp.concatenate([al_t[j], r_t[j]], axis=0), rhs)
        a_ab.append(jnp.where(strict, pm[:CHUNK, :BW], 0.0))
        a_ak.append(jnp.where(strict, pm[:CHUNK, BW:], 0.0))
        a_rb.append(jnp.where(lower, pm[CHUNK:, :BW], 0.0))
        a_rk.append(jnp.where(lower, pm[CHUNK:, BW:], 0.0))
        ends = jnp.concatenate([beta[sl] * e_rest, k[sl] * e_rest], axis=0).T
        b_eT.append(_rows_to_wide(ends, 0))
        v_blk.append(_block(v[sl], hmask))
        kv_c.append(_dot(_rows_to_wide(ends, CHUNK), v_blk[j]))
        col = jnp.exp(jnp.broadcast_to(c_last, (2 * CHUNK, BW)).T)
        e_col.append(_rows_to_wide(col, 0))
        yield
    avs = [_dot(jnp.concatenate([a_ak[j], a_rk[j]], axis=0), v_blk[j]) for j in range(n_chunk)]
    invs = yield from _unit_lower_inverses(a_ab, eye_wide, hmask)
    yield
    tws = [_dot(invs[j], jnp.concatenate([_block(al_t[j], hmask), _block(avs[j][:CHUNK], hmask)],
                                         axis=1)) for j in range(n_chunk)]
    t_al = [tw[:, :BW] for tw in tws]
    t_akv = [tw[:, BW:] for tw in tws]
    yield
    qzs = [_dot(a_rb[j], jnp.concatenate([_block(t_al[j], hmask), _block(t_akv[j], hmask)], axis=1))
           for j in range(n_chunk)]
    q_c = [r_t[j] + qzs[j][:, :BW] for j in range(n_chunk)]
    z_c = [qzs[j][:, BW:] + avs[j][CHUNK:] for j in range(n_chunk)]
    yield

    s = s_ref[...]
    ys = []
    for j in range(n_chunk):
        uo = _dot(jnp.concatenate([t_al[j], q_c[j]], axis=0), _block(s, hmask))
        ys.append(uo[CHUNK:] + z_c[j])
        uu = uo[:CHUNK] + t_akv[j]
        yield
        s = e_col[j] * s + _dot(b_eT[j], _block(uu, hmask)) + kv_c[j]
        yield
    s_ref[...] = s
    y = jnp.concatenate(ys, axis=0)

    mean = _head_sum(y, hsum) * (1.0 / HEAD)
    yc = y - mean
    var = _head_sum(yc * yc, hsum) * (1.0 / HEAD)
    y = yc * lax.rsqrt(var + RWKV_LN_EPS) * p["ln_g"][...] + p["ln_b"][...]
    bonus = _head_sum(r * k * p["rk"][...], hsum) * v
    return (y + bonus) * _silu(gate)


def _midpoint_rows(b, lvl, row):
    blk = CHUNK >> lvl
    half = blk // 2
    if blk >= 8:
        return jnp.concatenate(
            [jnp.broadcast_to(b[s + half - 1:s + half, :], (blk, b.shape[1]))
             for s in range(0, CHUNK, blk)], axis=0)
    p = row & (blk - 1)
    out = b
    for d in range(-(half - 1), half + 1):
        if d != 0:
            out = jnp.where(p == half - 1 + d, pltpu.roll(b, d % CHUNK, axis=0), out)
    return out


def _hgrn_seq(u, lb_ref, g_ref, tril_ref, hsum_ref, s_ref):
    tm = u.shape[0]
    q = u[:, 0:BW]
    kx = (1.0 - lb_ref[...]) * _sigmoid(-u[:, BW:2 * BW])
    log_f = jnp.log(1.0 - kx)
    v = u[:, 2 * BW:3 * BW]
    gate = u[:, 3 * BW:]
    b_all = _dot01_left(tril_ref[...], log_f)

    hmask = _head_lane_masks(hsum_ref[...])
    row = lax.broadcasted_iota(jnp.int32, (CHUNK, BW), 0)
    pos = lax.broadcasted_iota(jnp.int32, (CHUNK, BW), 1) & (HEAD - 1)
    masks = []
    for lvl in range(N_LEVEL):
        sh = N_LEVEL - lvl
        half = 1 << (sh - 1)
        same = (row >> sh) == (pos >> sh)
        masks.append(same & ((row & half) != 0) & ((pos & half) == 0))
    diag = row == pos

    yield
    parts = []
    for j in range(0, tm // CHUNK, 2):
        pair = []
        for jj in (j, j + 1):
            sl = slice(jj * CHUNK, (jj + 1) * CHUNK)
            qj, kj, vj, b = q[sl], kx[sl], v[sl], b_all[sl]
            b_last = b[CHUNK - 1:CHUNK, :]
            qb, kb = qj.astype(BF16), kj.astype(BF16)
            sc = jnp.where(diag, _dot_nt(qb, _block(kb, hmask)), 0.0)
            for lvl in range(N_LEVEL):
                e = jnp.exp(-jnp.abs(b - _midpoint_rows(b, lvl, row))).astype(BF16)
                sc = jnp.where(masks[lvl], _dot_nt(qb * e, _block(kb * e, hmask)), sc)
            col = jnp.exp(jnp.broadcast_to(b_last, (2 * CHUNK, BW)).T)
            pair.append((sc, qj * jnp.exp(b), _block(vj, hmask), kj * jnp.exp(b_last - b),
                         _rows_to_wide(col, 0)))
            yield
        k_eT = jnp.concatenate([pair[0][3], pair[1][3]], axis=0).T
        for i, (sc, q_in, v_blk, _, e_col) in enumerate(pair):
            kv = _dot(_rows_to_wide(k_eT, i * CHUNK), v_blk)
            parts.append((jnp.concatenate([sc, q_in], axis=1), v_blk, e_col, kv))
        yield
    s = s_ref[...]
    ys = []
    for lhs, v_blk, e_col, kv in parts:
        ys.append(_dot(lhs, jnp.concatenate([v_blk, _block(s, hmask)], axis=0)))
        s = e_col * s + kv
        yield
    s_ref[...] = s
    y = jnp.concatenate(ys, axis=0)
    ms = _head_sum(y * y, hsum_ref[...]) * (1.0 / HEAD)
    return y * lax.rsqrt(ms + EPS) * g_ref[...] * _silu(gate)


def _ret_tables(nt):
    dk, c = RET_QKHEAD, CHUNK
    pos = jnp.arange(nt, dtype=F32)
    inv_freq = 1.0 / (ROPE_BASE ** jnp.linspace(0.0, 1.0, dk // 2, dtype=F32))
    ang = pos[:, None] * inv_freq[None, :]
    cos, sin = jnp.cos(ang), jnp.sin(ang)
    cos_t = jnp.tile(jnp.concatenate([cos, cos], -1), (1, N_HEAD))
    sin_t = jnp.tile(jnp.concatenate([-sin, sin], -1), (1, N_HEAD))
    log_gamma = jnp.log1p(-jnp.exp2(-5.0 - jnp.arange(N_HEAD, dtype=F32)))
    idx = jnp.arange(c, dtype=F32)
    rel = idx[:, None] - idx[None, :]
    decay = jnp.where(rel >= 0, jnp.exp(log_gamma[:, None, None] * jnp.maximum(rel, 0.0)), 0.0)
    decay_wide = jnp.concatenate([decay[h] for h in range(N_HEAD)], axis=1)
    zeta = jnp.exp(log_gamma[:, None] * (c - 1 - idx))
    xi = jnp.exp(log_gamma[:, None] * (idx + 1.0))
    gamma_c = jnp.exp(log_gamma * c)
    lanes = lambda t: jnp.repeat(t.T, dk, axis=1)
    gam = jnp.broadcast_to(jnp.repeat(gamma_c, dk)[:, None], (RET_QK, BW))
    key_head = np.arange(RET_QK) // dk
    kmask = jnp.asarray((np.repeat(np.arange(N_HEAD), c)[:, None] == key_head[None, :])
                        .astype(np.float32), dtype=BF16)
    smask = jnp.asarray((key_head[:, None] == (np.arange(BW) // HEAD)[None, :]).astype(np.float32))
    return cos_t, sin_t, decay_wide, lanes(zeta), lanes(xi), gam, kmask, smask


def _ret_seq(u, cos_ref, sin_ref, dec_ref, zeta_ref, xi_ref, gam_ref, kmask_ref, smask_ref, g_ref,
             hsum_ref, s_ref):
    tm = u.shape[0]
    cos, sin = cos_ref[...], sin_ref[...]
    lane = lax.broadcasted_iota(jnp.int32, cos.shape, 1)
    first_half = (lane & (RET_QKHEAD // 2)) == 0

    def rope(t):
        half = RET_QKHEAD // 2
        swapped = jnp.where(first_half, pltpu.roll(t, RET_QK - half, axis=1),
                            pltpu.roll(t, half, axis=1))
        return t * cos + swapped * sin

    q = rope(u[:, 0:RET_QK])
    k = rope(u[:, RET_QK:2 * RET_QK]) * (RET_QKHEAD ** -0.5)
    v = u[:, 2 * RET_QK:2 * RET_QK + BW]
    gate = u[:, 2 * RET_QK + BW:]
    hmask = _head_lane_masks(hsum_ref[...])
    dec, xi, zeta = dec_ref[...], xi_ref[...], zeta_ref[...]
    kmask, smask, gam = kmask_ref[...], smask_ref[...], gam_ref[...]
    parts = []
    for j in range(tm // CHUNK):
        yield
        sl = slice(j * CHUNK, (j + 1) * CHUNK)
        qj, kj, vj = q[sl], k[sl], v[sl]
        k_blk = jnp.concatenate([kj.astype(BF16)] * N_HEAD, axis=0) * kmask
        sc = _dot_nt(qj, k_blk) * dec
        kv = smask * _dot_tn(kj * zeta, vj)
        parts.append((jnp.concatenate([sc, qj * xi], axis=1), _block(vj, hmask), kv))
    yield
    s = s_ref[...]
    ys = []
    for lhs, v_blk, kv in parts:
        ys.append(_dot(lhs, jnp.concatenate([v_blk, s.astype(BF16)], axis=0)))
        s = gam * s + kv
    s_ref[...] = s
    y = jnp.concatenate(ys, axis=0)
    ms = _head_sum(y * y, hsum_ref[...]) * (1.0 / HEAD)
    return y * lax.rsqrt(ms + EPS) * g_ref[...] * _silu(gate)


def _lru_seq(u, p, tail_ref, h_ref):
    tm = u.shape[0]
    x = u[:, :BW]
    gate = u[:, BW:]
    tail = tail_ref[...]
    cw = p["conv_w"][...]
    y = p["conv_b"][...] + cw[CONV_WIDTH - 1:CONV_WIDTH, :] * x
    for k in range(1, CONV_WIDTH):
        y = y + cw[CONV_WIDTH - 1 - k:CONV_WIDTH - k, :] * _shift_rows(x, k, tail)
    tail_ref[...] = x[tm - 8:, :]
    yield

    r = _sigmoid(_dot(y, p["wa"][...]) + p["ba"][...])
    i = _sigmoid(_dot(y, p["wx"][...]) + p["bx"][...])
    log_a = -LRU_C * r * _softplus(-p["lam"][...])
    a = jnp.exp(log_a)
    mult = jnp.sqrt(jnp.tanh(-log_a) * (a * a + 1.0))
    row = lax.broadcasted_iota(jnp.int32, x.shape, 0)
    mult = jnp.where(row + pl.program_id(1) * tm == 0, 1.0, mult)
    b = mult * (i * y)

    d = 1
    while d < tm:
        yield
        if d % 8:
            a_sh = jnp.where(row < d, 1.0, pltpu.roll(a, d, axis=0))
            b_sh = jnp.where(row < d, 0.0, pltpu.roll(b, d, axis=0))
            b = a * b_sh + b
            a = a * a_sh
        else:
            b = jnp.concatenate([b[:d], a[d:] * b[:tm - d] + b[d:]], axis=0)
            a = jnp.concatenate([a[:d], a[d:] * a[:tm - d]], axis=0)
        d *= 2
    h = a * h_ref[0:1, :] + b
    h_ref[0:1, :] = h[tm - 1:tm, :]
    return h * _silu(gate)


_RWKV_KEYS = ("mu", "w0", "w2", "a0", "a2", "kk", "ka", "rk", "ln_g", "ln_b")
_LRU_KEYS = ("conv_w", "conv_b", "wa", "ba", "wx", "bx", "lam")
_RET_KEYS = ("cos", "sin", "decay", "zeta", "xi", "gam", "kmask", "smask")
N_SCRATCH = 6


def _then_project(mixer, wb_ref, branch):
    y = yield from mixer
    yield
    return _dot(y, wb_ref[branch])


def _when_ready(inputs, key, make):
    while key not in inputs:
        yield
    return (yield from make(inputs[key]))


def _feed_seq(hn, w_in_ref, d, inputs):
    o = 0
    for key, n in enumerate((RWKV_W, HGRN_W, RET_W, LRU_W)):
        pieces = []
        for c0 in range(0, n, GATE_PIECE):
            w = min(GATE_PIECE, n - c0)
            pieces.append(jnp.dot(hn, w_in_ref[:, o + c0:o + c0 + w], preferred_element_type=F32))
            yield
        inputs[key] = jnp.concatenate(pieces, axis=1)
        o += n
    gates = []
    for bi in range(N_BRANCH):
        pieces = []
        for c0 in range(0, d, GATE_PIECE):
            col = MIX_W + bi * d + c0
            pieces.append(_sigmoid(jnp.dot(hn, w_in_ref[:, col:col + GATE_PIECE],
                                           preferred_element_type=F32).astype(BF16)))
            yield
        gates.append(jnp.concatenate(pieces, axis=1))
    return gates


def _layer_seq(i, x_ref, mod_ref, g_ref, w_in_ref, rwkv_p, tril_ref, hsum_ref, lb_ref, hg_ref,
               ret_t, rg_ref, lru_p, wb_ref, wo_ref, fg_ref, o_ref, scratch, final_norm):
    rw_tail, rw_s, hg_s, rt_s, lr_tail, lr_h = (ref.at[i] for ref in scratch)
    x = x_ref[i]
    d = x.shape[-1]
    mod = mod_ref[i]
    hn = _adaln(x, g_ref[...], mod).astype(BF16)
    inputs = {}
    mixers = (lambda u: _rwkv_seq(u, rwkv_p, tril_ref, hsum_ref, rw_tail, rw_s),
              lambda u: _hgrn_seq(u, lb_ref, hg_ref, tril_ref, hsum_ref, hg_s),
              lambda u: _ret_seq(u, *ret_t, rg_ref, hsum_ref, rt_s),
              lambda u: _lru_seq(u, lru_p, lr_tail, lr_h))
    gates, *projs = yield from _round_robin(
        [_feed_seq(hn, w_in_ref, d, inputs)]
        + [_then_project(_when_ready(inputs, bi, mixers[bi]), wb_ref, bi) for bi in range(N_BRANCH)])
    merged = None
    for bi in range(N_BRANCH):
        term = gates[bi] * projs[bi]
        merged = term if merged is None else merged + term
    yield
    out = x + mod[2:3, :] * _dot(merged, wo_ref[...])
    if final_norm:
        out = out * lax.rsqrt(jnp.mean(out * out, axis=-1, keepdims=True) + EPS) * fg_ref[...]
    o_ref[i] = out


def _layer_kernel(*refs, final_norm, stagger):
    it = iter(refs)
    take = lambda n: [next(it) for _ in range(n)]
    x_ref, mod_ref, g_ref, w_in_ref = take(4)
    rwkv_p = dict(zip(_RWKV_KEYS, take(len(_RWKV_KEYS))))
    tril_ref, hsum_ref, lb_ref, hg_ref = take(4)
    ret_t = take(len(_RET_KEYS))
    (rg_ref,) = take(1)
    lru_p = dict(zip(_LRU_KEYS, take(len(_LRU_KEYS))))
    wb_ref, wo_ref, fg_ref, o_ref = take(4)
    scratch = take(N_SCRATCH)

    @pl.when(pl.program_id(1) == 0)
    def _():
        for ref in scratch:
            ref[...] = jnp.zeros_like(ref)

    _run(_round_robin(
        [_layer_seq(i, x_ref, mod_ref, g_ref, w_in_ref, rwkv_p, tril_ref, hsum_ref, lb_ref, hg_ref,
                    ret_t, rg_ref, lru_p, wb_ref, wo_ref, fg_ref, o_ref, scratch, final_norm)
         for i in range(x_ref.shape[0])], stagger))


def _resident(shape, index_map):
    return pl.BlockSpec(shape, index_map, pipeline_mode=pl.Buffered(1))


def _layer(layer, h, mod, p, consts, final_norm):
    nb, nt, d = h.shape
    tm = min(TOKEN_TILE, nt)
    ns = SEQ_PER_STEP if nb % SEQ_PER_STEP == 0 else 1
    const = lambda a: _resident(a.shape, lambda b, t, nd=a.ndim: (0,) * nd)
    per_layer = lambda a: _resident((None,) + a.shape[1:],
                                    lambda b, t, nd=a.ndim: (layer,) + (0,) * (nd - 1))
    tok = pl.BlockSpec((ns, tm, d), lambda b, t: (b, t, 0))
    pos = pl.BlockSpec((tm, RET_QK), lambda b, t: (t, 0))
    ret = consts["ret"]
    operands = ([h, mod, p["norm_g"], p["w_in"]] + [p["rwkv"][k] for k in _RWKV_KEYS]
                + [consts["tril"], consts["hsum"], p["hgrn_lb"], p["hgrn_g"]]
                + [ret[k] for k in _RET_KEYS] + [p["ret_g"]] + [p["lru"][k] for k in _LRU_KEYS]
                + [p["w_branch"], p["w_out"], p["final_g"]])
    in_specs = ([tok, pl.BlockSpec((None, ns, 3, d), lambda b, t: (layer, b, 0, 0)),
                 per_layer(p["norm_g"]), per_layer(p["w_in"])]
                + [per_layer(p["rwkv"][k]) for k in _RWKV_KEYS]
                + [const(consts["tril"]), const(consts["hsum"]), per_layer(p["hgrn_lb"]),
                   per_layer(p["hgrn_g"]), pos, pos]
                + [const(ret[k]) for k in _RET_KEYS[2:]] + [per_layer(p["ret_g"])]
                + [per_layer(p["lru"][k]) for k in _LRU_KEYS]
                + [per_layer(p["w_branch"]), per_layer(p["w_out"]), const(p["final_g"])])
    scratch = [pltpu.VMEM((ns, 8, RWKV_SHIFT), F32), pltpu.VMEM((ns, HEAD, BW), F32),
               pltpu.VMEM((ns, HEAD, BW), F32), pltpu.VMEM((ns, RET_QK, BW), F32),
               pltpu.VMEM((ns, 8, BW), F32), pltpu.VMEM((ns, 8, BW), F32)]
    assert len(scratch) == N_SCRATCH
    return pl.pallas_call(
        functools.partial(_layer_kernel, final_norm=final_norm, stagger=LAYER_STAGGER),
        grid=(nb // ns, nt // tm),
        in_specs=in_specs,
        out_specs=tok,
        out_shape=jax.ShapeDtypeStruct((nb, nt, d), F32),
        scratch_shapes=scratch,
        compiler_params=pltpu.CompilerParams(dimension_semantics=("parallel", "arbitrary"),
                                             vmem_limit_bytes=VMEM_LIMIT),
        name="layer",
    )(*operands)


def _block_diag(w):
    depth, g, n, _ = w.shape
    eye = jnp.eye(g, dtype=w.dtype)
    return (w[:, :, :, None, :] * eye[None, :, None, :, None]).reshape(depth, g * n, g * n)


def kernel(x, c, norm_g, w_mod, b_mod, w_in, rwkv_mu, rwkv_w0, rwkv_w2, rwkv_a0, rwkv_a2, rwkv_kk, rwkv_ka, rwkv_rk, rwkv_ln_g, rwkv_ln_b, hgrn_lb, hgrn_norm_g, ret_norm_g, lru_conv_w, lru_conv_b, lru_wa, lru_ba, lru_wx, lru_bx, lru_lam, w_branch, w_out, final_g):
    nb, nt, d = x.shape
    depth = w_in.shape[0]
    tm = min(TOKEN_TILE, nt)
    assert nt % tm == 0 and tm % (2 * CHUNK) == 0 and CHUNK == HEAD
    assert w_in.shape[2] == MIX_W + N_BRANCH * d

    row3 = lambda a: a.reshape(depth, 1, -1)
    lb_p = jax.nn.softmax(hgrn_lb.astype(F32), axis=0)
    params = dict(
        norm_g=row3(norm_g), w_in=w_in.astype(BF16),
        rwkv=dict(mu=row3(rwkv_mu), w0=row3(rwkv_w0), w2=rwkv_w2.astype(BF16), a0=row3(rwkv_a0),
                  a2=rwkv_a2.astype(BF16), kk=row3(rwkv_kk), ka=row3(rwkv_ka), rk=row3(rwkv_rk),
                  ln_g=row3(rwkv_ln_g), ln_b=row3(rwkv_ln_b)),
        hgrn_lb=row3(jnp.cumsum(lb_p, axis=0) - lb_p[0]), hgrn_g=row3(hgrn_norm_g),
        ret_g=row3(ret_norm_g),
        lru=dict(conv_w=lru_conv_w, conv_b=row3(lru_conv_b), wa=_block_diag(lru_wa).astype(BF16),
                 ba=row3(lru_ba), wx=_block_diag(lru_wx).astype(BF16), bx=row3(lru_bx),
                 lam=row3(lru_lam)),
        w_branch=w_branch.astype(BF16), w_out=w_out.astype(BF16), final_g=final_g.reshape(1, d))

    head_id = np.arange(BW) // HEAD
    tok = np.arange(tm)
    consts = dict(
        hsum=jnp.asarray((head_id[:, None] == head_id[None, :]).astype(np.float32), dtype=BF16),
        tril=jnp.asarray(((tok[:, None] // CHUNK == tok[None, :] // CHUNK)
                          & (tok[None, :] <= tok[:, None])).astype(np.float32), dtype=BF16),
        ret=dict(zip(_RET_KEYS, _ret_tables(nt))))

    mod = _modulation(c, w_mod, b_mod).reshape(depth, nb, 3, d)
    h = x
    for l in range(depth):
        h = _layer(l, h, mod, params, consts, final_norm=(l == depth - 1))
    return h
```

```python
import functools

import numpy as np
import jax
import jax.numpy as jnp
from jax import lax
from jax.experimental import pallas as pl
from jax.experimental.pallas import tpu as pltpu

F32 = jnp.float32
BF16 = jnp.bfloat16

N_BRANCH = 4
BW = 256
N_HEAD = 4
HEAD = BW // N_HEAD
LANE = 128
EPS = 1e-6
RWKV_LORA = 64
RWKV_SHIFT = 3 * BW + 2 * RWKV_LORA
RWKV_W = RWKV_SHIFT + BW
RWKV_LN_EPS = 64e-5
HGRN_W = 4 * BW
RET_QKHEAD = HEAD // 2
RET_QK = N_HEAD * RET_QKHEAD
RET_W = 2 * RET_QK + 2 * BW
ROPE_BASE = 10000.0
LRU_W = 2 * BW
CONV_WIDTH = 4
LRU_C = 8.0
MIX_W = RWKV_W + HGRN_W + RET_W + LRU_W

CHUNK = 64
N_LEVEL = 6
TOKEN_TILE = 256
SEQ_PER_STEP = 2
GATE_PIECE = 256
LAYER_STAGGER = 10
VMEM_LIMIT = 56 * 1024 * 1024


def _dot(a, b):
    return jnp.dot(a.astype(BF16), b.astype(BF16), preferred_element_type=F32)


def _dot_nt(a, b):
    return lax.dot_general(a.astype(BF16), b.astype(BF16), (((1,), (1,)), ((), ())),
                           preferred_element_type=F32)


def _dot_tn(a, b):
    return lax.dot_general(a.astype(BF16), b.astype(BF16), (((0,), (0,)), ((), ())),
                           preferred_element_type=F32)


def _split2(x):
    hi = x.astype(BF16)
    lo = (x - hi.astype(F32)).astype(BF16)
    return hi, lo


def _dot_hi(a, b):
    ah, al = _split2(a)
    bh, bl = _split2(b)
    d = functools.partial(jnp.dot, preferred_element_type=F32)
    return d(ah, bh) + (d(ah, bl) + d(al, bh))


def _dot01_left(m01, x):
    d = functools.partial(jnp.dot, preferred_element_type=F32)
    hi, lo = _split2(x)
    return d(m01, hi) + d(m01, lo)


def _head_sum(x, m01):
    return jnp.dot(x.astype(BF16), m01, preferred_element_type=F32)


def _sigmoid(x):
    return 0.5 * jnp.tanh(0.5 * x) + 0.5


def _silu(x):
    return x * _sigmoid(x)


def _softplus(x):
    return jnp.maximum(x, 0.0) + jnp.log1p(jnp.exp(-jnp.abs(x)))


def _adaln(x, g, mod):
    ms = jnp.mean(x * x, axis=-1, keepdims=True)
    y = x * lax.rsqrt(ms + EPS) * g
    return y * (1.0 + mod[1:2, :]) + mod[0:1, :]


def _shift_rows(x, k, tail):
    xr = pltpu.roll(x, k, axis=0)
    row = lax.broadcasted_iota(jnp.int32, (8, x.shape[1]), 0)
    head = jnp.where(row < k, pltpu.roll(tail, k, axis=0), xr[0:8, :])
    return jnp.concatenate([head, xr[8:, :]], axis=0)


def _round_robin(gens, stagger=0):
    gens = list(gens)
    results = [None] * len(gens)
    alive = list(range(len(gens)))
    rounds = 0
    while alive:
        still = []
        for idx in alive:
            if rounds < idx * stagger:
                still.append(idx)
                continue
            try:
                next(gens[idx])
                still.append(idx)
            except StopIteration as stop:
                results[idx] = stop.value
        alive = still
        rounds += 1
        yield
    return results


def _run(gen):
    for _ in gen:
        pass


def _mod_kernel(c_ref, w_ref, b_ref, o_ref):
    o_ref[...] = _dot_hi(_silu(c_ref[...]), w_ref[...]) + b_ref[...]


def _modulation(c, w_mod, b_mod):
    depth, d, d3 = w_mod.shape
    nb = c.shape[0]
    n_col = d3 // d
    return pl.pallas_call(
        _mod_kernel,
        grid=(depth, n_col),
        in_specs=[pl.BlockSpec((nb, d), lambda l, j: (0, 0)),
                  pl.BlockSpec((None, d, d), lambda l, j: (l, 0, j)),
                  pl.BlockSpec((None, 1, d), lambda l, j: (l, 0, j))],
        out_specs=pl.BlockSpec((None, nb, d), lambda l, j: (l, 0, j)),
        out_shape=jax.ShapeDtypeStruct((depth, nb, d3), F32),
        name="modulation",
    )(c, w_mod, b_mod.reshape(depth, 1, d3))


def _head_lane_masks(bd01):
    return [bd01[i * HEAD:i * HEAD + CHUNK, 0:LANE] for i in range(LANE // HEAD)]


def _block(x, masks):
    xb = x.astype(BF16)
    zero = jnp.zeros((x.shape[0], LANE), BF16)
    per = LANE // HEAD
    rows = []
    for h in range(N_HEAD):
        g = h // per
        part = xb[:, g * LANE:(g + 1) * LANE] * masks[h % per]
        rows.append(jnp.concatenate([part if i == g else zero for i in range(BW // LANE)], axis=1))
    return jnp.concatenate(rows, axis=0)


def _rows_to_wide(xt, c0):
    return jnp.concatenate([xt[h * HEAD:(h + 1) * HEAD, c0:c0 + CHUNK] for h in range(N_HEAD)],
                           axis=1)


def _unit_lower_inverses(a_list, eye_wide, hmask):
    n = len(a_list)
    xs = [_dot(a, _block(a, hmask)) for a in a_list]
    invs = [eye_wide + a for a in a_list]
    for lvl in range(1, N_LEVEL):
        yield
        for i in range(n):
            xb = _block(xs[i], hmask)
            if lvl < N_LEVEL - 1:
                res = _dot(jnp.concatenate([xs[i], invs[i]], axis=0), xb)
                xs[i] = res[:CHUNK]
                invs[i] = invs[i] + res[CHUNK:]
            else:
                invs[i] = invs[i] + _dot(invs[i], xb)
    return invs


def _rwkv_seq(u, p, tril_ref, hsum_ref, tail_ref, s_ref):
    tm = u.shape[0]
    feats = u[:, :RWKV_SHIFT]
    gate = u[:, RWKV_SHIFT:]
    prev = _shift_rows(feats, 1, tail_ref[...])
    tail_ref[...] = feats[tm - 8:, :]
    x = feats + p["mu"][...] * (prev - feats)
    r = x[:, 0:BW]
    k = x[:, BW:2 * BW]
    v = x[:, 2 * BW:3 * BW]
    w_lo = x[:, 3 * BW:3 * BW + RWKV_LORA]
    a_lo = x[:, 3 * BW + RWKV_LORA:]

    hsum = hsum_ref[...]
    log_w = -float(np.exp(-0.5)) * _sigmoid(p["w0"][...] + _dot(jnp.tanh(w_lo), p["w2"][...]))
    a = _sigmoid(p["a0"][...] + _dot(a_lo, p["a2"][...]))
    kk = k * p["kk"][...]
    k = k * (1.0 + (a - 1.0) * p["ka"][...])
    kk = kk / jnp.maximum(jnp.sqrt(_head_sum(kk * kk, hsum)), 1e-12)
    beta = kk * a

    c = _dot01_left(tril_ref[...], log_w)
    n_chunk = tm // CHUNK
    hmask = _head_lane_masks(hsum)
    row = lax.broadcasted_iota(jnp.int32, (CHUNK, BW), 0)
    pos = lax.broadcasted_iota(jnp.int32, (CHUNK, BW), 1) & (HEAD - 1)
    strict = row > pos
    lower = row >= pos
    eye_wide = (row == pos).astype(F32)

    al_t, r_t, a_ab, a_ak, a_rb, a_rk, b_eT, kv_c, e_col, v_blk = ([] for _ in range(10))
    yield
    for j in range(n_chunk):
        sl = slice(j * CHUNK, (j + 1) * CHUNK)
        cj = c[sl]
        c_last = cj[CHUNK - 1:CHUNK, :]
        e_in = jnp.exp(cj)
        e_prev = jnp.exp(cj - log_w[sl])
        e_out = jnp.exp(-cj)
        e_rest = jnp.exp(c_last) * e_out
        al_t.append(-kk[sl] * e_prev)
        r_t.append(r[sl] * e_in)
        rhs = jnp.concatenate([_block(beta[sl] * e_out, hmask), _block(k[sl] * e_out, hmask)], axis=0)
        pm = _dot_nt(jnp.concatenate([al_t[j], r_t[j]], axis=0), rhs)
        a_ab.append(jnp.where(strict, pm[:CHUNK, :BW], 0.0))
        a_ak.append(jnp.where(strict, pm[:CHUNK, BW:], 0.0))
        a_rb.append(jnp.where(lower, pm[CHUNK:, :BW], 0.0))
        a_rk.append(jnp.where(lower, pm[CHUNK:, BW:], 0.0))
        ends = jnp.concatenate([beta[sl] * e_rest, k[sl] * e_rest], axis=0).T
        b_eT.append(_rows_to_wide(ends, 0))
        v_blk.append(_block(v[sl], hmask))
        kv_c.append(_dot(_rows_to_wide(ends, CHUNK), v_blk[j]))
        col = jnp.exp(jnp.broadcast_to(c_last, (2 * CHUNK, BW)).T)
        e_col.append(_rows_to_wide(col, 0))
        yield
    avs = [_dot(jnp.concatenate([a_ak[j], a_rk[j]], axis=0), v_blk[j]) for j in range(n_chunk)]
    invs = yield from _unit_lower_inverses(a_ab, eye_wide, hmask)
    yield
    tws = [_dot(invs[j], jnp.concatenate([_block(al_t[j], hmask), _block(avs[j][:CHUNK], hmask)],
                                         axis=1)) for j in range(n_chunk)]
    t_al = [tw[:, :BW] for tw in tws]
    t_akv = [tw[:, BW:] for tw in tws]
    yield
    qzs = [_dot(a_rb[j], jnp.concatenate([_block(t_al[j], hmask), _block(t_akv[j], hmask)], axis=1))
           for j in range(n_chunk)]
    q_c = [r_t[j] + qzs[j][:, :BW] for j in range(n_chunk)]
    z_c = [qzs[j][:, BW:] + avs[j][CHUNK:] for j in range(n_chunk)]
    yield

    s = s_ref[...]
    ys = []
    for j in range(n_chunk):
        uo = _dot(jnp.concatenate([t_al[j], q_c[j]], axis=0), _block(s, hmask))
        ys.append(uo[CHUNK:] + z_c[j])
        uu = uo[:CHUNK] + t_akv[j]
        yield
        s = e_col[j] * s + _dot(b_eT[j], _block(uu, hmask)) + kv_c[j]
        yield
    s_ref[...] = s
    y = jnp.concatenate(ys, axis=0)

    mean = _head_sum(y, hsum) * (1.0 / HEAD)
    yc = y - mean
    var = _head_sum(yc * yc, hsum) * (1.0 / HEAD)
    y = yc * lax.rsqrt(var + RWKV_LN_EPS) * p["ln_g"][...] + p["ln_b"][...]
    bonus = _head_sum(r * k * p["rk"][...], hsum) * v
    return (y + bonus) * _silu(gate)


def _midpoint_rows(b, lvl, row):
    blk = CHUNK >> lvl
    half = blk // 2
    if blk >= 8:
        return jnp.concatenate(
            [jnp.broadcast_to(b[s + half - 1:s + half, :], (blk, b.shape[1]))
             for s in range(0, CHUNK, blk)], axis=0)
    p = row & (blk - 1)
    out = b
    for d in range(-(half - 1), half + 1):
        if d != 0:
            out = jnp.where(p == half - 1 + d, pltpu.roll(b, d % CHUNK, axis=0), out)
    return out


def _hgrn_seq(u, lb_ref, g_ref, tril_ref, hsum_ref, s_ref):
    tm = u.shape[0]
    q = u[:, 0:BW]
    kx = (1.0 - lb_ref[...]) * _sigmoid(-u[:, BW:2 * BW])
    log_f = jnp.log(1.0 - kx)
    v = u[:, 2 * BW:3 * BW]
    gate = u[:, 3 * BW:]
    b_all = _dot01_left(tril_ref[...], log_f)

    hmask = _head_lane_masks(hsum_ref[...])
    row = lax.broadcasted_iota(jnp.int32, (CHUNK, BW), 0)
    pos = lax.broadcasted_iota(jnp.int32, (CHUNK, BW), 1) & (HEAD - 1)
    masks = []
    for lvl in range(N_LEVEL):
        sh = N_LEVEL - lvl
        half = 1 << (sh - 1)
        same = (row >> sh) == (pos >> sh)
        masks.append(same & ((row & half) != 0) & ((pos & half) == 0))
    diag = row == pos

    yield
    parts = []
    for j in range(0, tm // CHUNK, 2):
        pair = []
        for jj in (j, j + 1):
            sl = slice(jj * CHUNK, (jj + 1) * CHUNK)
            qj, kj, vj, b = q[sl], kx[sl], v[sl], b_all[sl]
            b_last = b[CHUNK - 1:CHUNK, :]
            qb, kb = qj.astype(BF16), kj.astype(BF16)
            sc = jnp.where(diag, _dot_nt(qb, _block(kb, hmask)), 0.0)
            for lvl in range(N_LEVEL):
                e = jnp.exp(-jnp.abs(b - _midpoint_rows(b, lvl, row))).astype(BF16)
                sc = jnp.where(masks[lvl], _dot_nt(qb * e, _block(kb * e, hmask)), sc)
            col = jnp.exp(jnp.broadcast_to(b_last, (2 * CHUNK, BW)).T)
            pair.append((sc, qj * jnp.exp(b), _block(vj, hmask), kj * jnp.exp(b_last - b),
                         _rows_to_wide(col, 0)))
            yield
        k_eT = jnp.concatenate([pair[0][3], pair[1][3]], axis=0).T
        for i, (sc, q_in, v_blk, _, e_col) in enumerate(pair):
            kv = _dot(_rows_to_wide(k_eT, i * CHUNK), v_blk)
            parts.append((jnp.concatenate([sc, q_in], axis=1), v_blk, e_col, kv))
        yield
    s = s_ref[...]
    ys = []
    for lhs, v_blk, e_col, kv in parts:
        ys.append(_dot(lhs, jnp.concatenate([v_blk, _block(s, hmask)], axis=0)))
        s = e_col * s + kv
        yield
    s_ref[...] = s
    y = jnp.concatenate(ys, axis=0)
    ms = _head_sum(y * y, hsum_ref[...]) * (1.0 / HEAD)
    return y * lax.rsqrt(ms + EPS) * g_ref[...] * _silu(gate)


def _ret_tables(nt):
    dk, c = RET_QKHEAD, CHUNK
    pos = jnp.arange(nt, dtype=F32)
    inv_freq = 1.0 / (ROPE_BASE ** jnp.linspace(0.0, 1.0, dk // 2, dtype=F32))
    ang = pos[:, None] * inv_freq[None, :]
    cos, sin = jnp.cos(ang), jnp.sin(ang)
    cos_t = jnp.tile(jnp.concatenate([cos, cos], -1), (1, N_HEAD))
    sin_t = jnp.tile(jnp.concatenate([-sin, sin], -1), (1, N_HEAD))
    log_gamma = jnp.log1p(-jnp.exp2(-5.0 - jnp.arange(N_HEAD, dtype=F32)))
    idx = jnp.arange(c, dtype=F32)
    rel = idx[:, None] - idx[None, :]
    decay = jnp.where(rel >= 0, jnp.exp(log_gamma[:, None, None] * jnp.maximum(rel, 0.0)), 0.0)
    decay_wide = jnp.concatenate([decay[h] for h in range(N_HEAD)], axis=1)
    zeta = jnp.exp(log_gamma[:, None] * (c - 1 - idx))
    xi = jnp.exp(log_gamma[:, None] * (idx + 1.0))
    gamma_c = jnp.exp(log_gamma * c)
    lanes = lambda t: jnp.repeat(t.T, dk, axis=1)
    gam = jnp.broadcast_to(jnp.repeat(gamma_c, dk)[:, None], (RET_QK, BW))
    key_head = np.arange(RET_QK) // dk
    kmask = jnp.asarray((np.repeat(np.arange(N_HEAD), c)[:, None] == key_head[None, :])
                        .astype(np.float32), dtype=BF16)
    smask = jnp.asarray((key_head[:, None] == (np.arange(BW) // HEAD)[None, :]).astype(np.float32))
    return cos_t, sin_t, decay_wide, lanes(zeta), lanes(xi), gam, kmask, smask


def _ret_seq(u, cos_ref, sin_ref, dec_ref, zeta_ref, xi_ref, gam_ref, kmask_ref, smask_ref, g_ref,
             hsum_ref, s_ref):
    tm = u.shape[0]
    cos, sin = cos_ref[...], sin_ref[...]
    lane = lax.broadcasted_iota(jnp.int32, cos.shape, 1)
    first_half = (lane & (RET_QKHEAD // 2)) == 0

    def rope(t):
        half = RET_QKHEAD // 2
        swapped = jnp.where(first_half, pltpu.roll(t, RET_QK - half, axis=1),
                            pltpu.roll(t, half, axis=1))
        return t * cos + swapped * sin

    q = rope(u[:, 0:RET_QK])
    k = rope(u[:, RET_QK:2 * RET_QK]) * (RET_QKHEAD ** -0.5)
    v = u[:, 2 * RET_QK:2 * RET_QK + BW]
    gate = u[:, 2 * RET_QK + BW:]
    hmask = _head_lane_masks(hsum_ref[...])
    dec, xi, zeta = dec_ref[...], xi_ref[...], zeta_ref[...]
    kmask, smask, gam = kmask_ref[...], smask_ref[...], gam_ref[...]
    parts = []
    for j in range(tm // CHUNK):
        yield
        sl = slice(j * CHUNK, (j + 1) * CHUNK)
        qj, kj, vj = q[sl], k[sl], v[sl]
        k_blk = jnp.concatenate([kj.astype(BF16)] * N_HEAD, axis=0) * kmask
        sc = _dot_nt(qj, k_blk) * dec
        kv = smask * _dot_tn(kj * zeta, vj)
        parts.append((jnp.concatenate([sc, qj * xi], axis=1), _block(vj, hmask), kv))
    yield
    s = s_ref[...]
    ys = []
    for lhs, v_blk, kv in parts:
        ys.append(_dot(lhs, jnp.concatenate([v_blk, s.astype(BF16)], axis=0)))
        s = gam * s + kv
    s_ref[...] = s
    y = jnp.concatenate(ys, axis=0)
    ms = _head_sum(y * y, hsum_ref[...]) * (1.0 / HEAD)
    return y * lax.rsqrt(ms + EPS) * g_ref[...] * _silu(gate)


def _lru_seq(u, p, tail_ref, h_ref):
    tm = u.shape[0]
    x = u[:, :BW]
    gate = u[:, BW:]
    tail = tail_ref[...]
    cw = p["conv_w"][...]
    y = p["conv_b"][...] + cw[CONV_WIDTH - 1:CONV_WIDTH, :] * x
    for k in range(1, CONV_WIDTH):
        y = y + cw[CONV_WIDTH - 1 - k:CONV_WIDTH - k, :] * _shift_rows(x, k, tail)
    tail_ref[...] = x[tm - 8:, :]
    yield

    r = _sigmoid(_dot(y, p["wa"][...]) + p["ba"][...])
    i = _sigmoid(_dot(y, p["wx"][...]) + p["bx"][...])
    log_a = -LRU_C * r * _softplus(-p["lam"][...])
    a = jnp.exp(log_a)
    mult = jnp.sqrt(jnp.tanh(-log_a) * (a * a + 1.0))
    row = lax.broadcasted_iota(jnp.int32, x.shape, 0)
    mult = jnp.where(row + pl.program_id(1) * tm == 0, 1.0, mult)
    b = mult * (i * y)

    d = 1
    while d < tm:
        yield
        if d % 8:
            a_sh = jnp.where(row < d, 1.0, pltpu.roll(a, d, axis=0))
            b_sh = jnp.where(row < d, 0.0, pltpu.roll(b, d, axis=0))
            b = a * b_sh + b
            a = a * a_sh
        else:
            b = jnp.concatenate([b[:d], a[d:] * b[:tm - d] + b[d:]], axis=0)
            a = jnp.concatenate([a[:d], a[d:] * a[:tm - d]], axis=0)
        d *= 2
    h = a * h_ref[0:1, :] + b
    h_ref[0:1, :] = h[tm - 1:tm, :]
    return h * _silu(gate)


_RWKV_KEYS = ("mu", "w0", "w2", "a0", "a2", "kk", "ka", "rk", "ln_g", "ln_b")
_LRU_KEYS = ("conv_w", "conv_b", "wa", "ba", "wx", "bx", "lam")
_RET_KEYS = ("cos", "sin", "decay", "zeta", "xi", "gam", "kmask", "smask")
_MATRIX_KEYS = ("w2", "a2", "conv_w", "wa", "wx")
N_SCRATCH = 6


def _then_project(mixer, wb_ref, branch):
    y = yield from mixer
    yield
    return _dot(y, wb_ref[branch])


def _when_ready(inputs, key, make):
    while key not in inputs:
        yield
    return (yield from make(inputs[key]))


def _feed_seq(hn, w_in_ref, d, inputs):
    o = 0
    for key, n in enumerate((RWKV_W, HGRN_W, RET_W, LRU_W)):
        pieces = []
        for c0 in range(0, n, GATE_PIECE):
            w = min(GATE_PIECE, n - c0)
            pieces.append(jnp.dot(hn, w_in_ref[:, o + c0:o + c0 + w], preferred_element_type=F32))
            yield
        inputs[key] = jnp.concatenate(pieces, axis=1)
        o += n
    gates = []
    for bi in range(N_BRANCH):
        pieces = []
        for c0 in range(0, d, GATE_PIECE):
            col = MIX_W + bi * d + c0
            pieces.append(_sigmoid(jnp.dot(hn, w_in_ref[:, col:col + GATE_PIECE],
                                           preferred_element_type=F32).astype(BF16)))
            yield
        gates.append(jnp.concatenate(pieces, axis=1))
    return gates


def _layer_seq(i, x_ref, mod_ref, g_ref, w_in_ref, rwkv_p, tril_ref, hsum_ref, lb_ref, hg_ref,
               ret_t, rg_ref, lru_p, wb_ref, wo_ref, fg_ref, o_ref, scratch, final_norm):
    rw_tail, rw_s, hg_s, rt_s, lr_tail, lr_h = (ref.at[i] for ref in scratch)
    x = x_ref[i]
    d = x.shape[-1]
    mod = mod_ref[i]
    hn = _adaln(x, g_ref[...], mod).astype(BF16)
    inputs = {}
    mixers = (lambda u: _rwkv_seq(u, rwkv_p, tril_ref, hsum_ref, rw_tail, rw_s),
              lambda u: _hgrn_seq(u, lb_ref, hg_ref, tril_ref, hsum_ref, hg_s),
              lambda u: _ret_seq(u, *ret_t, rg_ref, hsum_ref, rt_s),
              lambda u: _lru_seq(u, lru_p, lr_tail, lr_h))
    gates, *projs = yield from _round_robin(
        [_feed_seq(hn, w_in_ref, d, inputs)]
        + [_then_project(_when_ready(inputs, bi, mixers[bi]), wb_ref, bi) for bi in range(N_BRANCH)])
    merged = None
    for bi in range(N_BRANCH):
        term = gates[bi] * projs[bi]
        merged = term if merged is None else merged + term
    yield
    out = x + mod[2:3, :] * _dot(merged, wo_ref[...])
    if final_norm:
        out = out * lax.rsqrt(jnp.mean(out * out, axis=-1, keepdims=True) + EPS) * fg_ref[...]
    o_ref[i] = out


def _layer_kernel(*refs, layer, final_norm, stagger):
    it = iter(refs)
    take = lambda n: [next(it) for _ in range(n)]
    row = lambda ref: ref.at[layer:layer + 1]
    x_ref, mod_ref, g_ref, w_in_ref = take(4)
    rwkv_p = dict(zip(_RWKV_KEYS, take(len(_RWKV_KEYS))))
    tril_ref, hsum_ref, lb_ref, hg_ref = take(4)
    ret_t = take(len(_RET_KEYS))
    (rg_ref,) = take(1)
    lru_p = dict(zip(_LRU_KEYS, take(len(_LRU_KEYS))))
    wb_ref, wo_ref, fg_ref, o_ref = take(4)
    scratch = take(N_SCRATCH)
    g_ref, lb_ref, hg_ref, rg_ref = row(g_ref), row(lb_ref), row(hg_ref), row(rg_ref)
    rwkv_p = {k: v if k in _MATRIX_KEYS else row(v) for k, v in rwkv_p.items()}
    lru_p = {k: v if k in _MATRIX_KEYS else row(v) for k, v in lru_p.items()}

    @pl.when(pl.program_id(1) == 0)
    def _():
        for ref in scratch:
            ref[...] = jnp.zeros_like(ref)

    _run(_round_robin(
        [_layer_seq(i, x_ref, mod_ref, g_ref, w_in_ref, rwkv_p, tril_ref, hsum_ref, lb_ref, hg_ref,
                    ret_t, rg_ref, lru_p, wb_ref, wo_ref, fg_ref, o_ref, scratch, final_norm)
         for i in range(x_ref.shape[0])], stagger))


def _resident(shape, index_map):
    return pl.BlockSpec(shape, index_map, pipeline_mode=pl.Buffered(1))


def _layer(layer, h, mod, p, consts, final_norm):
    nb, nt, d = h.shape
    tm = min(TOKEN_TILE, nt)
    ns = SEQ_PER_STEP if nb % SEQ_PER_STEP == 0 else 1
    const = lambda a: _resident(a.shape, lambda b, t, nd=a.ndim: (0,) * nd)
    per_layer = lambda a: const(a) if a.ndim == 2 else _resident(
        (None,) + a.shape[1:], lambda b, t, nd=a.ndim: (layer,) + (0,) * (nd - 1))
    tok = pl.BlockSpec((ns, tm, d), lambda b, t: (b, t, 0))
    pos = pl.BlockSpec((tm, RET_QK), lambda b, t: (t, 0))
    ret = consts["ret"]
    operands = ([h, mod, p["norm_g"], p["w_in"]] + [p["rwkv"][k] for k in _RWKV_KEYS]
                + [consts["tril"], consts["hsum"], p["hgrn_lb"], p["hgrn_g"]]
                + [ret[k] for k in _RET_KEYS] + [p["ret_g"]] + [p["lru"][k] for k in _LRU_KEYS]
                + [p["w_branch"], p["w_out"], p["final_g"]])
    in_specs = ([tok, pl.BlockSpec((None, ns, 3, d), lambda b, t: (layer, b, 0, 0)),
                 per_layer(p["norm_g"]), per_layer(p["w_in"])]
                + [per_layer(p["rwkv"][k]) for k in _RWKV_KEYS]
                + [const(consts["tril"]), const(consts["hsum"]), per_layer(p["hgrn_lb"]),
                   per_layer(p["hgrn_g"]), pos, pos]
                + [const(ret[k]) for k in _RET_KEYS[2:]] + [per_layer(p["ret_g"])]
                + [per_layer(p["lru"][k]) for k in _LRU_KEYS]
                + [per_layer(p["w_branch"]), per_layer(p["w_out"]), const(p["final_g"])])
    scratch = [pltpu.VMEM((ns, 8, RWKV_SHIFT), F32), pltpu.VMEM((ns, HEAD, BW), F32),
               pltpu.VMEM((ns, HEAD, BW), F32), pltpu.VMEM((ns, RET_QK, BW), F32),
               pltpu.VMEM((ns, 8, BW), F32), pltpu.VMEM((ns, 8, BW), F32)]
    assert len(scratch) == N_SCRATCH
    return pl.pallas_call(
        functools.partial(_layer_kernel, layer=layer, final_norm=final_norm, stagger=LAYER_STAGGER),
        grid=(nb // ns, nt // tm),
        in_specs=in_specs,
        out_specs=tok,
        out_shape=jax.ShapeDtypeStruct((nb, nt, d), F32),
        scratch_shapes=scratch,
        compiler_params=pltpu.CompilerParams(dimension_semantics=("parallel", "arbitrary"),
                                             vmem_limit_bytes=VMEM_LIMIT),
        name="layer",
    )(*operands)


def _block_diag(w):
    depth, g, n, _ = w.shape
    eye = jnp.eye(g, dtype=w.dtype)
    return (w[:, :, :, None, :] * eye[None, :, None, :, None]).reshape(depth, g * n, g * n)


def kernel(x, c, norm_g, w_mod, b_mod, w_in, rwkv_mu, rwkv_w0, rwkv_w2, rwkv_a0, rwkv_a2, rwkv_kk, rwkv_ka, rwkv_rk, rwkv_ln_g, rwkv_ln_b, hgrn_lb, hgrn_norm_g, ret_norm_g, lru_conv_w, lru_conv_b, lru_wa, lru_ba, lru_wx, lru_bx, lru_lam, w_branch, w_out, final_g):
    nb, nt, d = x.shape
    depth = w_in.shape[0]
    tm = min(TOKEN_TILE, nt)
    assert nt % tm == 0 and tm % (2 * CHUNK) == 0 and CHUNK == HEAD
    assert w_in.shape[2] == MIX_W + N_BRANCH * d

    lb_p = jax.nn.softmax(hgrn_lb.astype(F32), axis=0)
    params = dict(
        norm_g=norm_g, w_in=w_in.astype(BF16),
        rwkv=dict(mu=rwkv_mu, w0=rwkv_w0, w2=rwkv_w2.astype(BF16), a0=rwkv_a0,
                  a2=rwkv_a2.astype(BF16), kk=rwkv_kk, ka=rwkv_ka, rk=rwkv_rk,
                  ln_g=rwkv_ln_g, ln_b=rwkv_ln_b),
        hgrn_lb=jnp.cumsum(lb_p, axis=0) - lb_p[0], hgrn_g=hgrn_norm_g, ret_g=ret_norm_g,
        lru=dict(conv_w=lru_conv_w, conv_b=lru_conv_b, wa=_block_diag(lru_wa).astype(BF16),
                 ba=lru_ba, wx=_block_diag(lru_wx).astype(BF16), bx=lru_bx, lam=lru_lam),
        w_branch=w_branch.astype(BF16), w_out=w_out.astype(BF16), final_g=final_g.reshape(1, d))

    head_id = np.arange(BW) // HEAD
    tok = np.arange(tm)
    consts = dict(
        hsum=jnp.asarray((head_id[:, None] == head_id[None, :]).astype(np.float32), dtype=BF16),
        tril=jnp.asarray(((tok[:, None] // CHUNK == tok[None, :] // CHUNK)
                          & (tok[None, :] <= tok[:, None])).astype(np.float32), dtype=BF16),
        ret=dict(zip(_RET_KEYS, _ret_tables(nt))))

    mod = _modulation(c, w_mod, b_mod).reshape(depth, nb, 3, d)
    h = x
    for l in range(depth):
        h = _layer(l, h, mod, params, consts, final_norm=(l == depth - 1))
    return h
```

```python
import functools

import numpy as np
import jax
import jax.numpy as jnp
from jax import lax
from jax.experimental import pallas as pl
from jax.experimental.pallas import tpu as pltpu

F32 = jnp.float32
BF16 = jnp.bfloat16

N_BRANCH = 4
BW = 256
N_HEAD = 4
HEAD = BW // N_HEAD
LANE = 128
EPS = 1e-6
RWKV_LORA = 64
RWKV_SHIFT = 3 * BW + 2 * RWKV_LORA
RWKV_W = RWKV_SHIFT + BW
RWKV_LN_EPS = 64e-5
HGRN_W = 4 * BW
RET_QKHEAD = HEAD // 2
RET_QK = N_HEAD * RET_QKHEAD
RET_W = 2 * RET_QK + 2 * BW
ROPE_BASE = 10000.0
LRU_W = 2 * BW
CONV_WIDTH = 4
LRU_C = 8.0
MIX_W = RWKV_W + HGRN_W + RET_W + LRU_W

CHUNK = 64
N_LEVEL = 6
TOKEN_TILE = 256
SEQ_PER_STEP = 2
FEED_PIECE = 256
SEQ_LAG = 4
VMEM_LIMIT = 56 * 1024 * 1024


def _dot(a, b):
    return jnp.dot(a.astype(BF16), b.astype(BF16), preferred_element_type=F32)


def _dot_nt(a, b):
    return lax.dot_general(a.astype(BF16), b.astype(BF16), (((1,), (1,)), ((), ())),
                           preferred_element_type=F32)


def _dot_tn(a, b):
    return lax.dot_general(a.astype(BF16), b.astype(BF16), (((0,), (0,)), ((), ())),
                           preferred_element_type=F32)


def _split2(x):
    hi = x.astype(BF16)
    lo = (x - hi.astype(F32)).astype(BF16)
    return hi, lo


def _dot_hi(a, b):
    ah, al = _split2(a)
    bh, bl = _split2(b)
    d = functools.partial(jnp.dot, preferred_element_type=F32)
    return d(ah, bh) + (d(ah, bl) + d(al, bh))


def _dot01_left(m01, x):
    d = functools.partial(jnp.dot, preferred_element_type=F32)
    hi, lo = _split2(x)
    return d(m01, hi) + d(m01, lo)


def _head_sum(x, m01):
    return jnp.dot(x.astype(BF16), m01, preferred_element_type=F32)


def _sigmoid(x):
    return 0.5 * jnp.tanh(0.5 * x) + 0.5


def _silu(x):
    return x * _sigmoid(x)


def _softplus(x):
    return jnp.maximum(x, 0.0) + jnp.log1p(jnp.exp(-jnp.abs(x)))


def _adaln(x, g, mod):
    ms = jnp.mean(x * x, axis=-1, keepdims=True)
    y = x * lax.rsqrt(ms + EPS) * g
    return y * (1.0 + mod[1:2, :]) + mod[0:1, :]


def _shift_rows(x, k, tail):
    xr = pltpu.roll(x, k, axis=0)
    row = lax.broadcasted_iota(jnp.int32, (8, x.shape[1]), 0)
    head = jnp.where(row < k, pltpu.roll(tail, k, axis=0), xr[0:8, :])
    return jnp.concatenate([head, xr[8:, :]], axis=0)


def _round_robin(gens):
    gens = list(gens)
    results = [None] * len(gens)
    alive = list(range(len(gens)))
    while alive:
        still = []
        for idx in alive:
            try:
                next(gens[idx])
                still.append(idx)
            except StopIteration as stop:
                results[idx] = stop.value
        alive = still
        yield
    return results


def _run(gen):
    try:
        while True:
            next(gen)
    except StopIteration as stop:
        return stop.value


def _mod_kernel(c_ref, w_ref, b_ref, o_ref):
    o_ref[...] = _dot_hi(_silu(c_ref[...]), w_ref[...]) + b_ref[...]


def _modulation(c, w_mod, b_mod):
    depth, d, d3 = w_mod.shape
    nb = c.shape[0]
    n_col = d3 // d
    return pl.pallas_call(
        _mod_kernel,
        grid=(depth, n_col),
        in_specs=[pl.BlockSpec((nb, d), lambda l, j: (0, 0)),
                  pl.BlockSpec((None, d, d), lambda l, j: (l, 0, j)),
                  pl.BlockSpec((None, 1, d), lambda l, j: (l, 0, j))],
        out_specs=pl.BlockSpec((None, nb, d), lambda l, j: (l, 0, j)),
        out_shape=jax.ShapeDtypeStruct((depth, nb, d3), F32),
        name="modulation",
    )(c, w_mod, b_mod.reshape(depth, 1, d3))


def _head_lane_masks(bd01):
    return [bd01[i * HEAD:i * HEAD + CHUNK, 0:LANE] for i in range(LANE // HEAD)]


def _block(x, masks):
    xb = x.astype(BF16)
    zero = jnp.zeros((x.shape[0], LANE), BF16)
    per = LANE // HEAD
    rows = []
    for h in range(N_HEAD):
        g = h // per
        part = xb[:, g * LANE:(g + 1) * LANE] * masks[h % per]
        rows.append(jnp.concatenate([part if i == g else zero for i in range(BW // LANE)], axis=1))
    return jnp.concatenate(rows, axis=0)


def _rows_to_wide(xt, c0):
    return jnp.concatenate([xt[h * HEAD:(h + 1) * HEAD, c0:c0 + CHUNK] for h in range(N_HEAD)],
                           axis=1)


def _unit_lower_inverses(a_list, eye_wide, hmask):
    n = len(a_list)
    xs = [_dot(a, _block(a, hmask)) for a in a_list]
    invs = [eye_wide + a for a in a_list]
    for lvl in range(1, N_LEVEL):
        yield
        for i in range(n):
            xb = _block(xs[i], hmask)
            if lvl < N_LEVEL - 1:
                res = _dot(jnp.concatenate([xs[i], invs[i]], axis=0), xb)
                xs[i] = res[:CHUNK]
                invs[i] = invs[i] + res[CHUNK:]
            else:
                invs[i] = invs[i] + _dot(invs[i], xb)
    return invs


def _rwkv_seq(u, p, tril_ref, hsum_ref, tail_ref, s_ref):
    tm = u.shape[0]
    feats = u[:, :RWKV_SHIFT]
    gate = u[:, RWKV_SHIFT:]
    prev = _shift_rows(feats, 1, tail_ref[...])
    tail_ref[...] = feats[tm - 8:, :]
    x = feats + p["mu"][...] * (prev - feats)
    r = x[:, 0:BW]
    k = x[:, BW:2 * BW]
    v = x[:, 2 * BW:3 * BW]
    w_lo = x[:, 3 * BW:3 * BW + RWKV_LORA]
    a_lo = x[:, 3 * BW + RWKV_LORA:]

    hsum = hsum_ref[...]
    log_w = -float(np.exp(-0.5)) * _sigmoid(p["w0"][...] + _dot(jnp.tanh(w_lo), p["w2"][...]))
    a = _sigmoid(p["a0"][...] + _dot(a_lo, p["a2"][...]))
    kk = k * p["kk"][...]
    k = k * (1.0 + (a - 1.0) * p["ka"][...])
    kk = kk / jnp.maximum(jnp.sqrt(_head_sum(kk * kk, hsum)), 1e-12)
    beta = kk * a

    c = _dot01_left(tril_ref[...], log_w)
    n_chunk = tm // CHUNK
    hmask = _head_lane_masks(hsum)
    row = lax.broadcasted_iota(jnp.int32, (CHUNK, BW), 0)
    pos = lax.broadcasted_iota(jnp.int32, (CHUNK, BW), 1) & (HEAD - 1)
    strict = row > pos
    lower = row >= pos
    eye_wide = (row == pos).astype(F32)

    al_t, r_t, a_ab, a_ak, a_rb, a_rk, b_eT, kv_c, e_col, v_blk = ([] for _ in range(10))
    yield
    for j in range(n_chunk):
        sl = slice(j * CHUNK, (j + 1) * CHUNK)
        cj = c[sl]
        c_last = cj[CHUNK - 1:CHUNK, :]
        e_in = jnp.exp(cj)
        e_prev = jnp.exp(cj - log_w[sl])
        e_out = jnp.exp(-cj)
        e_rest = jnp.exp(c_last) * e_out
        al_t.append(-kk[sl] * e_prev)
        r_t.append(r[sl] * e_in)
        rhs = jnp.concatenate([_block(beta[sl] * e_out, hmask), _block(k[sl] * e_out, hmask)], axis=0)
        pm = _dot_nt(jnp.concatenate([al_t[j], r_t[j]], axis=0), rhs)
        a_ab.append(jnp.where(strict, pm[:CHUNK, :BW], 0.0))
        a_ak.append(jnp.where(strict, pm[:CHUNK, BW:], 0.0))
        a_rb.append(jnp.where(lower, pm[CHUNK:, :BW], 0.0))
        a_rk.append(jnp.where(lower, pm[CHUNK:, BW:], 0.0))
        ends = jnp.concatenate([beta[sl] * e_rest, k[sl] * e_rest], axis=0).T
        b_eT.append(_rows_to_wide(ends, 0))
        v_blk.append(_block(v[sl], hmask))
        kv_c.append(_dot(_rows_to_wide(ends, CHUNK), v_blk[j]))
        col = jnp.exp(jnp.broadcast_to(c_last, (2 * CHUNK, BW)).T)
        e_col.append(_rows_to_wide(col, 0))
        yield
    avs = [_dot(jnp.concatenate([a_ak[j], a_rk[j]], axis=0), v_blk[j]) for j in range(n_chunk)]
    invs = yield from _unit_lower_inverses(a_ab, eye_wide, hmask)
    yield
    tws = [_dot(invs[j], jnp.concatenate([_block(al_t[j], hmask), _block(avs[j][:CHUNK], hmask)],
                                         axis=1)) for j in range(n_chunk)]
    t_al = [tw[:, :BW] for tw in tws]
    t_akv = [tw[:, BW:] for tw in tws]
    yield
    qzs = [_dot(a_rb[j], jnp.concatenate([_block(t_al[j], hmask), _block(t_akv[j], hmask)], axis=1))
           for j in range(n_chunk)]
    q_c = [r_t[j] + qzs[j][:, :BW] for j in range(n_chunk)]
    z_c = [qzs[j][:, BW:] + avs[j][CHUNK:] for j in range(n_chunk)]
    yield

    s = s_ref[...]
    ys = []
    for j in range(n_chunk):
        uo = _dot(jnp.concatenate([t_al[j], q_c[j]], axis=0), _block(s, hmask))
        ys.append(uo[CHUNK:] + z_c[j])
        uu = uo[:CHUNK] + t_akv[j]
        yield
        s = e_col[j] * s + _dot(b_eT[j], _block(uu, hmask)) + kv_c[j]
        yield
    s_ref[...] = s
    y = jnp.concatenate(ys, axis=0)

    mean = _head_sum(y, hsum) * (1.0 / HEAD)
    yc = y - mean
    var = _head_sum(yc * yc, hsum) * (1.0 / HEAD)
    y = yc * lax.rsqrt(var + RWKV_LN_EPS) * p["ln_g"][...] + p["ln_b"][...]
    bonus = _head_sum(r * k * p["rk"][...], hsum) * v
    return (y + bonus) * _silu(gate)


def _midpoint_rows(b, lvl, row):
    blk = CHUNK >> lvl
    half = blk // 2
    if blk >= 8:
        return jnp.concatenate(
            [jnp.broadcast_to(b[s + half - 1:s + half, :], (blk, b.shape[1]))
             for s in range(0, CHUNK, blk)], axis=0)
    p = row & (blk - 1)
    out = b
    for d in range(-(half - 1), half + 1):
        if d != 0:
            out = jnp.where(p == half - 1 + d, pltpu.roll(b, d % CHUNK, axis=0), out)
    return out


def _hgrn_seq(u, lb_ref, g_ref, tril_ref, hsum_ref, s_ref):
    tm = u.shape[0]
    q = u[:, 0:BW]
    kx = (1.0 - lb_ref[...]) * _sigmoid(-u[:, BW:2 * BW])
    log_f = jnp.log(1.0 - kx)
    v = u[:, 2 * BW:3 * BW]
    gate = u[:, 3 * BW:]
    b_all = _dot01_left(tril_ref[...], log_f)

    hmask = _head_lane_masks(hsum_ref[...])
    row = lax.broadcasted_iota(jnp.int32, (CHUNK, BW), 0)
    pos = lax.broadcasted_iota(jnp.int32, (CHUNK, BW), 1) & (HEAD - 1)
    masks = []
    for lvl in range(N_LEVEL):
        sh = N_LEVEL - lvl
        half = 1 << (sh - 1)
        same = (row >> sh) == (pos >> sh)
        masks.append(same & ((row & half) != 0) & ((pos & half) == 0))
    diag = row == pos

    yield
    parts = []
    for j in range(0, tm // CHUNK, 2):
        pair = []
        for jj in (j, j + 1):
            sl = slice(jj * CHUNK, (jj + 1) * CHUNK)
            qj, kj, vj, b = q[sl], kx[sl], v[sl], b_all[sl]
            b_last = b[CHUNK - 1:CHUNK, :]
            qb, kb = qj.astype(BF16), kj.astype(BF16)
            sc = jnp.where(diag, _dot_nt(qb, _block(kb, hmask)), 0.0)
            for lvl in range(N_LEVEL):
                e = jnp.exp(-jnp.abs(b - _midpoint_rows(b, lvl, row))).astype(BF16)
                sc = jnp.where(masks[lvl], _dot_nt(qb * e, _block(kb * e, hmask)), sc)
            col = jnp.exp(jnp.broadcast_to(b_last, (2 * CHUNK, BW)).T)
            pair.append((sc, qj * jnp.exp(b), _block(vj, hmask), kj * jnp.exp(b_last - b),
                         _rows_to_wide(col, 0)))
            yield
        k_eT = jnp.concatenate([pair[0][3], pair[1][3]], axis=0).T
        for i, (sc, q_in, v_blk, _, e_col) in enumerate(pair):
            kv = _dot(_rows_to_wide(k_eT, i * CHUNK), v_blk)
            parts.append((jnp.concatenate([sc, q_in], axis=1), v_blk, e_col, kv))
        yield
    s = s_ref[...]
    ys = []
    for lhs, v_blk, e_col, kv in parts:
        ys.append(_dot(lhs, jnp.concatenate([v_blk, _block(s, hmask)], axis=0)))
        s = e_col * s + kv
        yield
    s_ref[...] = s
    y = jnp.concatenate(ys, axis=0)
    ms = _head_sum(y * y, hsum_ref[...]) * (1.0 / HEAD)
    return y * lax.rsqrt(ms + EPS) * g_ref[...] * _silu(gate)


def _ret_tables(nt):
    dk, c = RET_QKHEAD, CHUNK
    pos = jnp.arange(nt, dtype=F32)
    inv_freq = 1.0 / (ROPE_BASE ** jnp.linspace(0.0, 1.0, dk // 2, dtype=F32))
    ang = pos[:, None] * inv_freq[None, :]
    cos, sin = jnp.cos(ang), jnp.sin(ang)
    cos_t = jnp.tile(jnp.concatenate([cos, cos], -1), (1, N_HEAD))
    sin_t = jnp.tile(jnp.concatenate([-sin, sin], -1), (1, N_HEAD))
    log_gamma = jnp.log1p(-jnp.exp2(-5.0 - jnp.arange(N_HEAD, dtype=F32)))
    idx = jnp.arange(c, dtype=F32)
    rel = idx[:, None] - idx[None, :]
    decay = jnp.where(rel >= 0, jnp.exp(log_gamma[:, None, None] * jnp.maximum(rel, 0.0)), 0.0)
    decay_wide = jnp.concatenate([decay[h] for h in range(N_HEAD)], axis=1)
    zeta = jnp.exp(log_gamma[:, None] * (c - 1 - idx))
    xi = jnp.exp(log_gamma[:, None] * (idx + 1.0))
    gamma_c = jnp.exp(log_gamma * c)
    lanes = lambda t: jnp.repeat(t.T, dk, axis=1)
    gam = jnp.broadcast_to(jnp.repeat(gamma_c, dk)[:, None], (RET_QK, BW))
    key_head = np.arange(RET_QK) // dk
    kmask = jnp.asarray((np.repeat(np.arange(N_HEAD), c)[:, None] == key_head[None, :])
                        .astype(np.float32), dtype=BF16)
    smask = jnp.asarray((key_head[:, None] == (np.arange(BW) // HEAD)[None, :]).astype(np.float32))
    return cos_t, sin_t, decay_wide, lanes(zeta), lanes(xi), gam, kmask, smask


def _ret_seq(u, cos_ref, sin_ref, dec_ref, zeta_ref, xi_ref, gam_ref, kmask_ref, smask_ref, g_ref,
             hsum_ref, s_ref):
    tm = u.shape[0]
    cos, sin = cos_ref[...], sin_ref[...]
    lane = lax.broadcasted_iota(jnp.int32, cos.shape, 1)
    first_half = (lane & (RET_QKHEAD // 2)) == 0

    def rope(t):
        half = RET_QKHEAD // 2
        swapped = jnp.where(first_half, pltpu.roll(t, RET_QK - half, axis=1),
                            pltpu.roll(t, half, axis=1))
        return t * cos + swapped * sin

    q = rope(u[:, 0:RET_QK])
    k = rope(u[:, RET_QK:2 * RET_QK]) * (RET_QKHEAD ** -0.5)
    v = u[:, 2 * RET_QK:2 * RET_QK + BW]
    gate = u[:, 2 * RET_QK + BW:]
    hmask = _head_lane_masks(hsum_ref[...])
    dec, xi, zeta = dec_ref[...], xi_ref[...], zeta_ref[...]
    kmask, smask, gam = kmask_ref[...], smask_ref[...], gam_ref[...]
    parts = []
    for j in range(tm // CHUNK):
        yield
        sl = slice(j * CHUNK, (j + 1) * CHUNK)
        qj, kj, vj = q[sl], k[sl], v[sl]
        k_blk = jnp.concatenate([kj.astype(BF16)] * N_HEAD, axis=0) * kmask
        sc = _dot_nt(qj, k_blk) * dec
        kv = smask * _dot_tn(kj * zeta, vj)
        parts.append((jnp.concatenate([sc, qj * xi], axis=1), _block(vj, hmask), kv))
    yield
    s = s_ref[...]
    ys = []
    for lhs, v_blk, kv in parts:
        ys.append(_dot(lhs, jnp.concatenate([v_blk, s.astype(BF16)], axis=0)))
        s = gam * s + kv
    s_ref[...] = s
    y = jnp.concatenate(ys, axis=0)
    ms = _head_sum(y * y, hsum_ref[...]) * (1.0 / HEAD)
    return y * lax.rsqrt(ms + EPS) * g_ref[...] * _silu(gate)


def _lru_seq(u, p, tail_ref, h_ref):
    tm = u.shape[0]
    x = u[:, :BW]
    gate = u[:, BW:]
    tail = tail_ref[...]
    cw = p["conv_w"][...]
    y = p["conv_b"][...] + cw[CONV_WIDTH - 1:CONV_WIDTH, :] * x
    for k in range(1, CONV_WIDTH):
        y = y + cw[CONV_WIDTH - 1 - k:CONV_WIDTH - k, :] * _shift_rows(x, k, tail)
    tail_ref[...] = x[tm - 8:, :]
    yield

    r = _sigmoid(_dot(y, p["wa"][...]) + p["ba"][...])
    i = _sigmoid(_dot(y, p["wx"][...]) + p["bx"][...])
    log_a = -LRU_C * r * _softplus(-p["lam"][...])
    a = jnp.exp(log_a)
    mult = jnp.sqrt(jnp.tanh(-log_a) * (a * a + 1.0))
    row = lax.broadcasted_iota(jnp.int32, x.shape, 0)
    mult = jnp.where(row + pl.program_id(1) * tm == 0, 1.0, mult)
    b = mult * (i * y)

    d = 1
    while d < tm:
        yield
        if d % 8:
            a_sh = jnp.where(row < d, 1.0, pltpu.roll(a, d, axis=0))
            b_sh = jnp.where(row < d, 0.0, pltpu.roll(b, d, axis=0))
            b = a * b_sh + b
            a = a * a_sh
        else:
            b = jnp.concatenate([b[:d], a[d:] * b[:tm - d] + b[d:]], axis=0)
            a = jnp.concatenate([a[:d], a[d:] * a[:tm - d]], axis=0)
        d *= 2
    h = a * h_ref[0:1, :] + b
    h_ref[0:1, :] = h[tm - 1:tm, :]
    return h * _silu(gate)


_RWKV_KEYS = ("mu", "w0", "w2", "a0", "a2", "kk", "ka", "rk", "ln_g", "ln_b")
_LRU_KEYS = ("conv_w", "conv_b", "wa", "ba", "wx", "bx", "lam")
_RET_KEYS = ("cos", "sin", "decay", "zeta", "xi", "gam", "kmask", "smask")
_MATRIX_KEYS = ("w2", "a2", "conv_w", "wa", "wx")
N_SCRATCH = 6


def _then_project(mixer, wb_ref, branch):
    y = yield from mixer
    yield
    return _dot(y, wb_ref[branch])


def _when_ready(inputs, key, make, lag):
    while key not in inputs:
        yield
    for _ in range(lag):
        yield
    return (yield from make(inputs[key]))


def _feed_all(hns, w_in_ref, d, inputs):
    tm = hns[0].shape[0]
    hn = jnp.concatenate(hns, axis=0)
    per_seq = lambda a: [a[i * tm:(i + 1) * tm] for i in range(len(hns))]
    o = 0
    for key, n in enumerate((RWKV_W, HGRN_W, RET_W, LRU_W)):
        pieces = []
        for c0 in range(0, n, FEED_PIECE):
            w = min(FEED_PIECE, n - c0)
            pieces.append(jnp.dot(hn, w_in_ref[:, o + c0:o + c0 + w], preferred_element_type=F32))
            yield
        for i, u in enumerate(per_seq(jnp.concatenate(pieces, axis=1))):
            inputs[i][key] = u
        o += n
    gates = [[] for _ in hns]
    for bi in range(N_BRANCH):
        pieces = []
        for c0 in range(0, d, FEED_PIECE):
            col = MIX_W + bi * d + c0
            pieces.append(_sigmoid(jnp.dot(hn, w_in_ref[:, col:col + FEED_PIECE],
                                           preferred_element_type=F32).astype(BF16)))
            yield
        for i, g in enumerate(per_seq(jnp.concatenate(pieces, axis=1))):
            gates[i].append(g)
    return gates


def _layer_kernel(*refs, layer, final_norm):
    it = iter(refs)
    take = lambda n: [next(it) for _ in range(n)]
    row = lambda ref: ref.at[layer:layer + 1]
    x_ref, mod_ref, g_ref, w_in_ref = take(4)
    rwkv_p = dict(zip(_RWKV_KEYS, take(len(_RWKV_KEYS))))
    tril_ref, hsum_ref, lb_ref, hg_ref = take(4)
    ret_t = take(len(_RET_KEYS))
    (rg_ref,) = take(1)
    lru_p = dict(zip(_LRU_KEYS, take(len(_LRU_KEYS))))
    wb_ref, wo_ref, fg_ref, o_ref = take(4)
    scratch = take(N_SCRATCH)
    g_ref, lb_ref, hg_ref, rg_ref = row(g_ref), row(lb_ref), row(hg_ref), row(rg_ref)
    rwkv_p = {k: v if k in _MATRIX_KEYS else row(v) for k, v in rwkv_p.items()}
    lru_p = {k: v if k in _MATRIX_KEYS else row(v) for k, v in lru_p.items()}

    @pl.when(pl.program_id(1) == 0)
    def _():
        for ref in scratch:
            ref[...] = jnp.zeros_like(ref)

    ns, _, d = x_ref.shape
    xs = [x_ref[i] for i in range(ns)]
    mods = [mod_ref[i] for i in range(ns)]
    hns = [_adaln(xs[i], g_ref[...], mods[i]).astype(BF16) for i in range(ns)]

    def mixers(i):
        rw_tail, rw_s, hg_s, rt_s, lr_tail, lr_h = (ref.at[i] for ref in scratch)
        return (lambda u: _rwkv_seq(u, rwkv_p, tril_ref, hsum_ref, rw_tail, rw_s),
                lambda u: _hgrn_seq(u, lb_ref, hg_ref, tril_ref, hsum_ref, hg_s),
                lambda u: _ret_seq(u, *ret_t, rg_ref, hsum_ref, rt_s),
                lambda u: _lru_seq(u, lru_p, lr_tail, lr_h))

    inputs = [{} for _ in range(ns)]
    gates, *projs = _run(_round_robin(
        [_feed_all(hns, w_in_ref, d, inputs)]
        + [_then_project(_when_ready(inputs[i], bi, mixers(i)[bi], i * SEQ_LAG), wb_ref, bi)
           for bi in range(N_BRANCH) for i in range(ns)]))
    merged = []
    for i in range(ns):
        terms = [gates[i][bi] * projs[bi * ns + i] for bi in range(N_BRANCH)]
        merged.append((terms[0] + terms[1]) + (terms[2] + terms[3]))
    update = _dot(jnp.concatenate(merged, axis=0), wo_ref[...])
    tm = xs[0].shape[0]
    for i in range(ns):
        out = xs[i] + mods[i][2:3, :] * update[i * tm:(i + 1) * tm]
        if final_norm:
            out = out * lax.rsqrt(jnp.mean(out * out, axis=-1, keepdims=True) + EPS) * fg_ref[...]
        o_ref[i] = out


def _resident(shape, index_map):
    return pl.BlockSpec(shape, index_map, pipeline_mode=pl.Buffered(1))


def _layer(layer, h, mod, p, consts, final_norm):
    nb, nt, d = h.shape
    tm = min(TOKEN_TILE, nt)
    ns = SEQ_PER_STEP if nb % SEQ_PER_STEP == 0 else 1
    const = lambda a: _resident(a.shape, lambda b, t, nd=a.ndim: (0,) * nd)
    per_layer = lambda a: const(a) if a.ndim == 2 else _resident(
        (None,) + a.shape[1:], lambda b, t, nd=a.ndim: (layer,) + (0,) * (nd - 1))
    tok = pl.BlockSpec((ns, tm, d), lambda b, t: (b, t, 0))
    pos = pl.BlockSpec((tm, RET_QK), lambda b, t: (t, 0))
    ret = consts["ret"]
    operands = ([h, mod, p["norm_g"], p["w_in"]] + [p["rwkv"][k] for k in _RWKV_KEYS]
                + [consts["tril"], consts["hsum"], p["hgrn_lb"], p["hgrn_g"]]
                + [ret[k] for k in _RET_KEYS] + [p["ret_g"]] + [p["lru"][k] for k in _LRU_KEYS]
                + [p["w_branch"], p["w_out"], p["final_g"]])
    in_specs = ([tok, pl.BlockSpec((None, ns, 3, d), lambda b, t: (layer, b, 0, 0)),
                 per_layer(p["norm_g"]), per_layer(p["w_in"])]
                + [per_layer(p["rwkv"][k]) for k in _RWKV_KEYS]
                + [const(consts["tril"]), const(consts["hsum"]), per_layer(p["hgrn_lb"]),
                   per_layer(p["hgrn_g"]), pos, pos]
                + [const(ret[k]) for k in _RET_KEYS[2:]] + [per_layer(p["ret_g"])]
                + [per_layer(p["lru"][k]) for k in _LRU_KEYS]
                + [per_layer(p["w_branch"]), per_layer(p["w_out"]), const(p["final_g"])])
    scratch = [pltpu.VMEM((ns, 8, RWKV_SHIFT), F32), pltpu.VMEM((ns, HEAD, BW), F32),
               pltpu.VMEM((ns, HEAD, BW), F32), pltpu.VMEM((ns, RET_QK, BW), F32),
               pltpu.VMEM((ns, 8, BW), F32), pltpu.VMEM((ns, 8, BW), F32)]
    assert len(scratch) == N_SCRATCH
    return pl.pallas_call(
        functools.partial(_layer_kernel, layer=layer, final_norm=final_norm),
        grid=(nb // ns, nt // tm),
        in_specs=in_specs,
        out_specs=tok,
        out_shape=jax.ShapeDtypeStruct((nb, nt, d), F32),
        scratch_shapes=scratch,
        compiler_params=pltpu.CompilerParams(dimension_semantics=("parallel", "arbitrary"),
                                             vmem_limit_bytes=VMEM_LIMIT),
        name="layer",
    )(*operands)


def _block_diag(w):
    depth, g, n, _ = w.shape
    eye = jnp.eye(g, dtype=w.dtype)
    return (w[:, :, :, None, :] * eye[None, :, None, :, None]).reshape(depth, g * n, g * n)


def kernel(x, c, norm_g, w_mod, b_mod, w_in, rwkv_mu, rwkv_w0, rwkv_w2, rwkv_a0, rwkv_a2, rwkv_kk, rwkv_ka, rwkv_rk, rwkv_ln_g, rwkv_ln_b, hgrn_lb, hgrn_norm_g, ret_norm_g, lru_conv_w, lru_conv_b, lru_wa, lru_ba, lru_wx, lru_bx, lru_lam, w_branch, w_out, final_g):
    nb, nt, d = x.shape
    depth = w_in.shape[0]
    tm = min(TOKEN_TILE, nt)
    assert nt % tm == 0 and tm % (2 * CHUNK) == 0 and CHUNK == HEAD
    assert w_in.shape[2] == MIX_W + N_BRANCH * d

    lb_p = jax.nn.softmax(hgrn_lb.astype(F32), axis=0)
    params = dict(
        norm_g=norm_g, w_in=w_in.astype(BF16),
        rwkv=dict(mu=rwkv_mu, w0=rwkv_w0, w2=rwkv_w2.astype(BF16), a0=rwkv_a0,
                  a2=rwkv_a2.astype(BF16), kk=rwkv_kk, ka=rwkv_ka, rk=rwkv_rk,
                  ln_g=rwkv_ln_g, ln_b=rwkv_ln_b),
        hgrn_lb=jnp.cumsum(lb_p, axis=0) - lb_p[0], hgrn_g=hgrn_norm_g, ret_g=ret_norm_g,
        lru=dict(conv_w=lru_conv_w, conv_b=lru_conv_b, wa=_block_diag(lru_wa).astype(BF16),
                 ba=lru_ba, wx=_block_diag(lru_wx).astype(BF16), bx=lru_bx, lam=lru_lam),
        w_branch=w_branch.astype(BF16), w_out=w_out.astype(BF16), final_g=final_g.reshape(1, d))

    head_id = np.arange(BW) // HEAD
    tok = np.arange(tm)
    consts = dict(
        hsum=jnp.asarray((head_id[:, None] == head_id[None, :]).astype(np.float32), dtype=BF16),
        tril=jnp.asarray(((tok[:, None] // CHUNK == tok[None, :] // CHUNK)
                          & (tok[None, :] <= tok[:, None])).astype(np.float32), dtype=BF16),
        ret=dict(zip(_RET_KEYS, _ret_tables(nt))))

    mod = _modulation(c, w_mod, b_mod).reshape(depth, nb, 3, d)
    h = x
    for l in range(depth):
        h = _layer(l, h, mod, params, consts, final_norm=(l == depth - 1))
    return h
```

```python
import functools

import numpy as np
import jax
import jax.numpy as jnp
from jax import lax
from jax.experimental import pallas as pl
from jax.experimental.pallas import tpu as pltpu

F32 = jnp.float32
BF16 = jnp.bfloat16

N_BRANCH = 4
BW = 256
N_HEAD = 4
HEAD = BW // N_HEAD
LANE = 128
EPS = 1e-6
RWKV_LORA = 64
RWKV_SHIFT = 3 * BW + 2 * RWKV_LORA
RWKV_W = RWKV_SHIFT + BW
RWKV_LN_EPS = 64e-5
HGRN_W = 4 * BW
RET_QKHEAD = HEAD // 2
RET_QK = N_HEAD * RET_QKHEAD
RET_W = 2 * RET_QK + 2 * BW
ROPE_BASE = 10000.0
LRU_W = 2 * BW
CONV_WIDTH = 4
LRU_C = 8.0
MIX_W = RWKV_W + HGRN_W + RET_W + LRU_W

CHUNK = 64
N_LEVEL = 6
TOKEN_TILE = 256
SEQ_PER_STEP = 2
FEED_PIECE = 256
SEQ_LAG = 4
VMEM_LIMIT = 56 * 1024 * 1024


def _dot(a, b):
    return jnp.dot(a.astype(BF16), b.astype(BF16), preferred_element_type=F32)


def _dot_nt(a, b):
    return lax.dot_general(a.astype(BF16), b.astype(BF16), (((1,), (1,)), ((), ())),
                           preferred_element_type=F32)


def _dot_tn(a, b):
    return lax.dot_general(a.astype(BF16), b.astype(BF16), (((0,), (0,)), ((), ())),
                           preferred_element_type=F32)


def _split2(x):
    hi = x.astype(BF16)
    lo = (x - hi.astype(F32)).astype(BF16)
    return hi, lo


def _dot_hi(a, b):
    ah, al = _split2(a)
    bh, bl = _split2(b)
    d = functools.partial(jnp.dot, preferred_element_type=F32)
    return d(ah, bh) + (d(ah, bl) + d(al, bh))


def _dot01_left(m01, x):
    d = functools.partial(jnp.dot, preferred_element_type=F32)
    hi, lo = _split2(x)
    return d(m01, hi) + d(m01, lo)


def _head_sum(x, m01):
    return jnp.dot(x.astype(BF16), m01, preferred_element_type=F32)


def _sigmoid(x):
    return 0.5 * jnp.tanh(0.5 * x) + 0.5


def _silu(x):
    return x * _sigmoid(x)


def _softplus(x):
    return jnp.maximum(x, 0.0) + jnp.log1p(jnp.exp(-jnp.abs(x)))


def _adaln(x, g, mod):
    ms = jnp.mean(x * x, axis=-1, keepdims=True)
    y = x * lax.rsqrt(ms + EPS) * g
    return y * (1.0 + mod[1:2, :]) + mod[0:1, :]


def _shift_rows(x, k, tail):
    xr = pltpu.roll(x, k, axis=0)
    row = lax.broadcasted_iota(jnp.int32, (8, x.shape[1]), 0)
    head = jnp.where(row < k, pltpu.roll(tail, k, axis=0), xr[0:8, :])
    return jnp.concatenate([head, xr[8:, :]], axis=0)


def _round_robin(gens):
    gens = list(gens)
    results = [None] * len(gens)
    alive = list(range(len(gens)))
    while alive:
        still = []
        for idx in alive:
            try:
                next(gens[idx])
                still.append(idx)
            except StopIteration as stop:
                results[idx] = stop.value
        alive = still
        yield
    return results


def _run(gen):
    try:
        while True:
            next(gen)
    except StopIteration as stop:
        return stop.value


def _mod_kernel(c_ref, w_ref, b_ref, o_ref):
    o_ref[...] = _dot_hi(_silu(c_ref[...]), w_ref[...]) + b_ref[...]


def _modulation(c, w_mod, b_mod):
    depth, d, d3 = w_mod.shape
    nb = c.shape[0]
    n_col = d3 // d
    return pl.pallas_call(
        _mod_kernel,
        grid=(depth, n_col),
        in_specs=[pl.BlockSpec((nb, d), lambda l, j: (0, 0)),
                  pl.BlockSpec((None, d, d), lambda l, j: (l, 0, j)),
                  pl.BlockSpec((None, 1, d), lambda l, j: (l, 0, j))],
        out_specs=pl.BlockSpec((None, nb, d), lambda l, j: (l, 0, j)),
        out_shape=jax.ShapeDtypeStruct((depth, nb, d3), F32),
        name="modulation",
    )(c, w_mod, b_mod.reshape(depth, 1, d3))


def _head_lane_masks(bd01):
    return [bd01[i * HEAD:i * HEAD + CHUNK, 0:LANE] for i in range(LANE // HEAD)]


def _block(x, masks):
    xb = x.astype(BF16)
    zero = jnp.zeros((x.shape[0], LANE), BF16)
    per = LANE // HEAD
    rows = []
    for h in range(N_HEAD):
        g = h // per
        part = xb[:, g * LANE:(g + 1) * LANE] * masks[h % per]
        rows.append(jnp.concatenate([part if i == g else zero for i in range(BW // LANE)], axis=1))
    return jnp.concatenate(rows, axis=0)


def _rows_to_wide(xt, c0):
    return jnp.concatenate([xt[h * HEAD:(h + 1) * HEAD, c0:c0 + CHUNK] for h in range(N_HEAD)],
                           axis=1)


def _unit_lower_inverses(a_list, eye_wide, hmask):
    n = len(a_list)
    xs = [_dot(a, _block(a, hmask)) for a in a_list]
    invs = [eye_wide + a for a in a_list]
    for lvl in range(1, N_LEVEL):
        yield
        for i in range(n):
            xb = _block(xs[i], hmask)
            if lvl < N_LEVEL - 1:
                res = _dot(jnp.concatenate([xs[i], invs[i]], axis=0), xb)
                xs[i] = res[:CHUNK]
                invs[i] = invs[i] + res[CHUNK:]
            else:
                invs[i] = invs[i] + _dot(invs[i], xb)
    return invs


def _rwkv_seq(u, p, tril_ref, hsum_ref, tail_ref, s_ref):
    tm = u.shape[0]
    feats = u[:, :RWKV_SHIFT]
    gate = u[:, RWKV_SHIFT:]
    prev = _shift_rows(feats, 1, tail_ref[...])
    tail_ref[...] = feats[tm - 8:, :]
    x = feats + p["mu"][...] * (prev - feats)
    r = x[:, 0:BW]
    k = x[:, BW:2 * BW]
    v = x[:, 2 * BW:3 * BW]
    w_lo = x[:, 3 * BW:3 * BW + RWKV_LORA]
    a_lo = x[:, 3 * BW + RWKV_LORA:]

    hsum = hsum_ref[...]
    log_w = -float(np.exp(-0.5)) * _sigmoid(p["w0"][...] + _dot(jnp.tanh(w_lo), p["w2"][...]))
    a = _sigmoid(p["a0"][...] + _dot(a_lo, p["a2"][...]))
    kk = k * p["kk"][...]
    k = k * (1.0 + (a - 1.0) * p["ka"][...])
    kk = kk / jnp.maximum(jnp.sqrt(_head_sum(kk * kk, hsum)), 1e-12)
    beta = kk * a

    c = _dot01_left(tril_ref[...], log_w)
    n_chunk = tm // CHUNK
    hmask = _head_lane_masks(hsum)
    row = lax.broadcasted_iota(jnp.int32, (CHUNK, BW), 0)
    pos = lax.broadcasted_iota(jnp.int32, (CHUNK, BW), 1) & (HEAD - 1)
    strict = row > pos
    lower = row >= pos
    eye_wide = (row == pos).astype(F32)

    al_t, r_t, a_ab, a_ak, a_rb, a_rk, b_eT, kv_c, e_col, v_blk = ([] for _ in range(10))
    yield
    for j in range(n_chunk):
        sl = slice(j * CHUNK, (j + 1) * CHUNK)
        cj = c[sl]
        c_last = cj[CHUNK - 1:CHUNK, :]
        e_in = jnp.exp(cj)
        e_prev = jnp.exp(cj - log_w[sl])
        e_out = jnp.exp(-cj)
        e_rest = jnp.exp(c_last) * e_out
        al_t.append(-kk[sl] * e_prev)
        r_t.append(r[sl] * e_in)
        rhs = jnp.concatenate([_block(beta[sl] * e_out, hmask), _block(k[sl] * e_out, hmask)], axis=0)
        pm = _dot_nt(jnp.concatenate([al_t[j], r_t[j]], axis=0), rhs)
        a_ab.append(jnp.where(strict, pm[:CHUNK, :BW], 0.0))
        a_ak.append(jnp.where(strict, pm[:CHUNK, BW:], 0.0))
        a_rb.append(jnp.where(lower, pm[CHUNK:, :BW], 0.0))
        a_rk.append(jnp.where(lower, pm[CHUNK:, BW:], 0.0))
        ends = jnp.concatenate([beta[sl] * e_rest, k[sl] * e_rest], axis=0).T
        b_eT.append(_rows_to_wide(ends, 0))
        v_blk.append(_block(v[sl], hmask))
        kv_c.append(_dot(_rows_to_wide(ends, CHUNK), v_blk[j]))
        col = jnp.exp(jnp.broadcast_to(c_last, (2 * CHUNK, BW)).T)
        e_col.append(_rows_to_wide(col, 0))
        yield
    avs = [_dot(jnp.concatenate([a_ak[j], a_rk[j]], axis=0), v_blk[j]) for j in range(n_chunk)]
    invs = yield from _unit_lower_inverses(a_ab, eye_wide, hmask)
    yield
    tws = [_dot(invs[j], jnp.concatenate([_block(al_t[j], hmask), _block(avs[j][:CHUNK], hmask)],
                                         axis=1)) for j in range(n_chunk)]
    t_al = [tw[:, :BW] for tw in tws]
    t_akv = [tw[:, BW:] for tw in tws]
    yield
    qzs = [_dot(a_rb[j], jnp.concatenate([_block(t_al[j], hmask), _block(t_akv[j], hmask)], axis=1))
           for j in range(n_chunk)]
    q_c = [r_t[j] + qzs[j][:, :BW] for j in range(n_chunk)]
    z_c = [qzs[j][:, BW:] + avs[j][CHUNK:] for j in range(n_chunk)]
    yield

    s = s_ref[...]
    ys = []
    for j in range(n_chunk):
        uo = _dot(jnp.concatenate([t_al[j], q_c[j]], axis=0), _block(s, hmask))
        ys.append(uo[CHUNK:] + z_c[j])
        uu = uo[:CHUNK] + t_akv[j]
        yield
        s = e_col[j] * s + _dot(b_eT[j], _block(uu, hmask)) + kv_c[j]
        yield
    s_ref[...] = s
    y = jnp.concatenate(ys, axis=0)

    mean = _head_sum(y, hsum) * (1.0 / HEAD)
    yc = y - mean
    var = _head_sum(yc * yc, hsum) * (1.0 / HEAD)
    y = yc * lax.rsqrt(var + RWKV_LN_EPS) * p["ln_g"][...] + p["ln_b"][...]
    bonus = _head_sum(r * k * p["rk"][...], hsum) * v
    return (y + bonus) * _silu(gate)


def _midpoint_rows(b, lvl, row):
    blk = CHUNK >> lvl
    half = blk // 2
    if blk >= 8:
        return jnp.concatenate(
            [jnp.broadcast_to(b[s + half - 1:s + half, :], (blk, b.shape[1]))
             for s in range(0, CHUNK, blk)], axis=0)
    p = row & (blk - 1)
    out = b
    for d in range(-(half - 1), half + 1):
        if d != 0:
            out = jnp.where(p == half - 1 + d, pltpu.roll(b, d % CHUNK, axis=0), out)
    return out


def _hgrn_seq(u, lb_ref, g_ref, tril_ref, hsum_ref, s_ref):
    tm = u.shape[0]
    q = u[:, 0:BW]
    kx = (1.0 - lb_ref[...]) * _sigmoid(-u[:, BW:2 * BW])
    log_f = jnp.log(1.0 - kx)
    v = u[:, 2 * BW:3 * BW]
    gate = u[:, 3 * BW:]
    b_all = _dot01_left(tril_ref[...], log_f)

    hmask = _head_lane_masks(hsum_ref[...])
    row = lax.broadcasted_iota(jnp.int32, (CHUNK, BW), 0)
    pos = lax.broadcasted_iota(jnp.int32, (CHUNK, BW), 1) & (HEAD - 1)
    masks = []
    for lvl in range(N_LEVEL):
        sh = N_LEVEL - lvl
        half = 1 << (sh - 1)
        same = (row >> sh) == (pos >> sh)
        masks.append(same & ((row & half) != 0) & ((pos & half) == 0))
    diag = row == pos

    yield
    parts = []
    for j in range(0, tm // CHUNK, 2):
        pair = []
        for jj in (j, j + 1):
            sl = slice(jj * CHUNK, (jj + 1) * CHUNK)
            qj, kj, vj, b = q[sl], kx[sl], v[sl], b_all[sl]
            b_last = b[CHUNK - 1:CHUNK, :]
            qb, kb = qj.astype(BF16), kj.astype(BF16)
            sc = jnp.where(diag, _dot_nt(qb, _block(kb, hmask)), 0.0)
            for lvl in range(N_LEVEL):
                e = jnp.exp(-jnp.abs(b - _midpoint_rows(b, lvl, row))).astype(BF16)
                sc = jnp.where(masks[lvl], _dot_nt(qb * e, _block(kb * e, hmask)), sc)
            col = jnp.exp(jnp.broadcast_to(b_last, (2 * CHUNK, BW)).T)
            pair.append((sc, qj * jnp.exp(b), _block(vj, hmask), kj * jnp.exp(b_last - b),
                         _rows_to_wide(col, 0)))
            yield
        k_eT = jnp.concatenate([pair[0][3], pair[1][3]], axis=0).T
        for i, (sc, q_in, v_blk, _, e_col) in enumerate(pair):
            kv = _dot(_rows_to_wide(k_eT, i * CHUNK), v_blk)
            parts.append((jnp.concatenate([sc, q_in], axis=1), v_blk, e_col, kv))
        yield
    s = s_ref[...]
    ys = []
    for lhs, v_blk, e_col, kv in parts:
        ys.append(_dot(lhs, jnp.concatenate([v_blk, _block(s, hmask)], axis=0)))
        s = e_col * s + kv
        yield
    s_ref[...] = s
    y = jnp.concatenate(ys, axis=0)
    ms = _head_sum(y * y, hsum_ref[...]) * (1.0 / HEAD)
    return y * lax.rsqrt(ms + EPS) * g_ref[...] * _silu(gate)


def _ret_tables(nt):
    dk, c = RET_QKHEAD, CHUNK
    pos = jnp.arange(nt, dtype=F32)
    inv_freq = 1.0 / (ROPE_BASE ** jnp.linspace(0.0, 1.0, dk // 2, dtype=F32))
    ang = pos[:, None] * inv_freq[None, :]
    cos, sin = jnp.cos(ang), jnp.sin(ang)
    cos_t = jnp.tile(jnp.concatenate([cos, cos], -1), (1, N_HEAD))
    sin_t = jnp.tile(jnp.concatenate([-sin, sin], -1), (1, N_HEAD))
    log_gamma = jnp.log1p(-jnp.exp2(-5.0 - jnp.arange(N_HEAD, dtype=F32)))
    idx = jnp.arange(c, dtype=F32)
    rel = idx[:, None] - idx[None, :]
    decay = jnp.where(rel >= 0, jnp.exp(log_gamma[:, None, None] * jnp.maximum(rel, 0.0)), 0.0)
    decay_wide = jnp.concatenate([decay[h] for h in range(N_HEAD)], axis=1)
    zeta = jnp.exp(log_gamma[:, None] * (c - 1 - idx))
    xi = jnp.exp(log_gamma[:, None] * (idx + 1.0))
    gamma_c = jnp.exp(log_gamma * c)
    lanes = lambda t: jnp.repeat(t.T, dk, axis=1)
    gam = jnp.broadcast_to(jnp.repeat(gamma_c, dk)[:, None], (RET_QK, BW))
    key_head = np.arange(RET_QK) // dk
    kmask = jnp.asarray((np.repeat(np.arange(N_HEAD), c)[:, None] == key_head[None, :])
                        .astype(np.float32), dtype=BF16)
    smask = jnp.asarray((key_head[:, None] == (np.arange(BW) // HEAD)[None, :]).astype(np.float32))
    return cos_t, sin_t, decay_wide, lanes(zeta), lanes(xi), gam, kmask, smask


def _ret_seq(u, cos_ref, sin_ref, dec_ref, zeta_ref, xi_ref, gam_ref, kmask_ref, smask_ref, g_ref,
             hsum_ref, s_ref):
    tm = u.shape[0]
    cos, sin = cos_ref[...], sin_ref[...]
    lane = lax.broadcasted_iota(jnp.int32, cos.shape, 1)
    first_half = (lane & (RET_QKHEAD // 2)) == 0

    def rope(t):
        half = RET_QKHEAD // 2
        swapped = jnp.where(first_half, pltpu.roll(t, RET_QK - half, axis=1),
                            pltpu.roll(t, half, axis=1))
        return t * cos + swapped * sin

    q = rope(u[:, 0:RET_QK])
    k = rope(u[:, RET_QK:2 * RET_QK]) * (RET_QKHEAD ** -0.5)
    v = u[:, 2 * RET_QK:2 * RET_QK + BW]
    gate = u[:, 2 * RET_QK + BW:]
    hmask = _head_lane_masks(hsum_ref[...])
    dec, xi, zeta = dec_ref[...], xi_ref[...], zeta_ref[...]
    kmask, smask, gam = kmask_ref[...], smask_ref[...], gam_ref[...]
    parts = []
    for j in range(tm // CHUNK):
        yield
        sl = slice(j * CHUNK, (j + 1) * CHUNK)
        qj, kj, vj = q[sl], k[sl], v[sl]
        k_blk = jnp.concatenate([kj.astype(BF16)] * N_HEAD, axis=0) * kmask
        sc = _dot_nt(qj, k_blk) * dec
        kv = smask * _dot_tn(kj * zeta, vj)
        parts.append((jnp.concatenate([sc, qj * xi], axis=1), _block(vj, hmask), kv))
    yield
    s = s_ref[...]
    ys = []
    for lhs, v_blk, kv in parts:
        ys.append(_dot(lhs, jnp.concatenate([v_blk, s.astype(BF16)], axis=0)))
        s = gam * s + kv
    s_ref[...] = s
    y = jnp.concatenate(ys, axis=0)
    ms = _head_sum(y * y, hsum_ref[...]) * (1.0 / HEAD)
    return y * lax.rsqrt(ms + EPS) * g_ref[...] * _silu(gate)


def _lru_seq(u, p, tail_ref, h_ref):
    tm = u.shape[0]
    x = u[:, :BW]
    gate = u[:, BW:]
    tail = tail_ref[...]
    cw = p["conv_w"][...]
    y = p["conv_b"][...] + cw[CONV_WIDTH - 1:CONV_WIDTH, :] * x
    for k in range(1, CONV_WIDTH):
        y = y + cw[CONV_WIDTH - 1 - k:CONV_WIDTH - k, :] * _shift_rows(x, k, tail)
    tail_ref[...] = x[tm - 8:, :]
    yield

    r = _sigmoid(_dot(y, p["wa"][...]) + p["ba"][...])
    i = _sigmoid(_dot(y, p["wx"][...]) + p["bx"][...])
    log_a = -LRU_C * r * _softplus(-p["lam"][...])
    a = jnp.exp(log_a)
    mult = jnp.sqrt(jnp.tanh(-log_a) * (a * a + 1.0))
    row = lax.broadcasted_iota(jnp.int32, x.shape, 0)
    mult = jnp.where(row + pl.program_id(1) * tm == 0, 1.0, mult)
    b = mult * (i * y)

    d = 1
    while d < tm:
        yield
        if d % 8:
            a_sh = jnp.where(row < d, 1.0, pltpu.roll(a, d, axis=0))
            b_sh = jnp.where(row < d, 0.0, pltpu.roll(b, d, axis=0))
            b = a * b_sh + b
            a = a * a_sh
        else:
            b = jnp.concatenate([b[:d], a[d:] * b[:tm - d] + b[d:]], axis=0)
            a = jnp.concatenate([a[:d], a[d:] * a[:tm - d]], axis=0)
        d *= 2
    h = a * h_ref[0:1, :] + b
    h_ref[0:1, :] = h[tm - 1:tm, :]
    return h * _silu(gate)


_RWKV_KEYS = ("mu", "w0", "w2", "a0", "a2", "kk", "ka", "rk", "ln_g", "ln_b")
_LRU_KEYS = ("conv_w", "conv_b", "wa", "ba", "wx", "bx", "lam")
_RET_KEYS = ("cos", "sin", "decay", "zeta", "xi", "gam", "kmask", "smask")
_MATRIX_KEYS = ("w2", "a2", "conv_w", "wa", "wx")
N_SCRATCH = 6


def _then_project(mixer, wb_ref, branch):
    y = yield from mixer
    yield
    return _dot(y, wb_ref[branch]).astype(BF16)


def _when_ready(inputs, key, make, lag):
    while key not in inputs:
        yield
    for _ in range(lag):
        yield
    return (yield from make(inputs[key]))


def _feed_all(hns, w_in_ref, d, inputs):
    tm = hns[0].shape[0]
    hn = jnp.concatenate(hns, axis=0)
    per_seq = lambda a: [a[i * tm:(i + 1) * tm] for i in range(len(hns))]
    o = 0
    for key, n in enumerate((RWKV_W, HGRN_W, RET_W, LRU_W)):
        pieces = []
        for c0 in range(0, n, FEED_PIECE):
            w = min(FEED_PIECE, n - c0)
            pieces.append(jnp.dot(hn, w_in_ref[:, o + c0:o + c0 + w], preferred_element_type=F32))
            yield
        for i, u in enumerate(per_seq(jnp.concatenate(pieces, axis=1))):
            inputs[i][key] = u
        o += n
    gates = [[] for _ in hns]
    for bi in range(N_BRANCH):
        pieces = []
        for c0 in range(0, d, FEED_PIECE):
            col = MIX_W + bi * d + c0
            pieces.append(_sigmoid(jnp.dot(hn, w_in_ref[:, col:col + FEED_PIECE],
                                           preferred_element_type=F32).astype(BF16)))
            yield
        for i, g in enumerate(per_seq(jnp.concatenate(pieces, axis=1))):
            gates[i].append(g)
    return gates


def _layer_kernel(*refs, layer, final_norm):
    it = iter(refs)
    take = lambda n: [next(it) for _ in range(n)]
    row = lambda ref: ref.at[layer:layer + 1]
    x_ref, mod_ref, g_ref, w_in_ref = take(4)
    rwkv_p = dict(zip(_RWKV_KEYS, take(len(_RWKV_KEYS))))
    tril_ref, hsum_ref, lb_ref, hg_ref = take(4)
    ret_t = take(len(_RET_KEYS))
    (rg_ref,) = take(1)
    lru_p = dict(zip(_LRU_KEYS, take(len(_LRU_KEYS))))
    wb_ref, wo_ref, fg_ref, o_ref = take(4)
    scratch = take(N_SCRATCH)
    g_ref, lb_ref, hg_ref, rg_ref = row(g_ref), row(lb_ref), row(hg_ref), row(rg_ref)
    rwkv_p = {k: v if k in _MATRIX_KEYS else row(v) for k, v in rwkv_p.items()}
    lru_p = {k: v if k in _MATRIX_KEYS else row(v) for k, v in lru_p.items()}

    @pl.when(pl.program_id(1) == 0)
    def _():
        for ref in scratch:
            ref[...] = jnp.zeros_like(ref)

    ns, _, d = x_ref.shape
    xs = [x_ref[i] for i in range(ns)]
    mods = [mod_ref[i] for i in range(ns)]
    hns = [_adaln(xs[i], g_ref[...], mods[i]).astype(BF16) for i in range(ns)]

    def mixers(i):
        rw_tail, rw_s, hg_s, rt_s, lr_tail, lr_h = (ref.at[i] for ref in scratch)
        return (lambda u: _rwkv_seq(u, rwkv_p, tril_ref, hsum_ref, rw_tail, rw_s),
                lambda u: _hgrn_seq(u, lb_ref, hg_ref, tril_ref, hsum_ref, hg_s),
                lambda u: _ret_seq(u, *ret_t, rg_ref, hsum_ref, rt_s),
                lambda u: _lru_seq(u, lru_p, lr_tail, lr_h))

    inputs = [{} for _ in range(ns)]
    gates, *projs = _run(_round_robin(
        [_feed_all(hns, w_in_ref, d, inputs)]
        + [_then_project(_when_ready(inputs[i], bi, mixers(i)[bi], i * SEQ_LAG), wb_ref, bi)
           for bi in range(N_BRANCH) for i in range(ns)]))
    merged = []
    for i in range(ns):
        terms = [gates[i][bi] * projs[bi * ns + i] for bi in range(N_BRANCH)]
        merged.append((terms[0] + terms[1]) + (terms[2] + terms[3]))
    update = _dot(jnp.concatenate(merged, axis=0), wo_ref[...])
    tm = xs[0].shape[0]
    for i in range(ns):
        out = xs[i] + mods[i][2:3, :] * update[i * tm:(i + 1) * tm]
        if final_norm:
            out = out * lax.rsqrt(jnp.mean(out * out, axis=-1, keepdims=True) + EPS) * fg_ref[...]
        o_ref[i] = out


def _resident(shape, index_map):
    return pl.BlockSpec(shape, index_map, pipeline_mode=pl.Buffered(1))


def _layer(layer, h, mod, p, consts, final_norm):
    nb, nt, d = h.shape
    tm = min(TOKEN_TILE, nt)
    ns = SEQ_PER_STEP if nb % SEQ_PER_STEP == 0 else 1
    const = lambda a: _resident(a.shape, lambda b, t, nd=a.ndim: (0,) * nd)
    per_layer = lambda a: const(a) if a.ndim == 2 else _resident(
        (None,) + a.shape[1:], lambda b, t, nd=a.ndim: (layer,) + (0,) * (nd - 1))
    tok = pl.BlockSpec((ns, tm, d), lambda b, t: (b, t, 0))
    pos = pl.BlockSpec((tm, RET_QK), lambda b, t: (t, 0))
    ret = consts["ret"]
    operands = ([h, mod, p["norm_g"], p["w_in"]] + [p["rwkv"][k] for k in _RWKV_KEYS]
                + [consts["tril"], consts["hsum"], p["hgrn_lb"], p["hgrn_g"]]
                + [ret[k] for k in _RET_KEYS] + [p["ret_g"]] + [p["lru"][k] for k in _LRU_KEYS]
                + [p["w_branch"], p["w_out"], p["final_g"]])
    in_specs = ([tok, pl.BlockSpec((None, ns, 3, d), lambda b, t: (layer, b, 0, 0)),
                 per_layer(p["norm_g"]), per_layer(p["w_in"])]
                + [per_layer(p["rwkv"][k]) for k in _RWKV_KEYS]
                + [const(consts["tril"]), const(consts["hsum"]), per_layer(p["hgrn_lb"]),
                   per_layer(p["hgrn_g"]), pos, pos]
                + [const(ret[k]) for k in _RET_KEYS[2:]] + [per_layer(p["ret_g"])]
                + [per_layer(p["lru"][k]) for k in _LRU_KEYS]
                + [per_layer(p["w_branch"]), per_layer(p["w_out"]), const(p["final_g"])])
    scratch = [pltpu.VMEM((ns, 8, RWKV_SHIFT), F32), pltpu.VMEM((ns, HEAD, BW), F32),
               pltpu.VMEM((ns, HEAD, BW), F32), pltpu.VMEM((ns, RET_QK, BW), F32),
               pltpu.VMEM((ns, 8, BW), F32), pltpu.VMEM((ns, 8, BW), F32)]
    assert len(scratch) == N_SCRATCH
    return pl.pallas_call(
        functools.partial(_layer_kernel, layer=layer, final_norm=final_norm),
        grid=(nb // ns, nt // tm),
        in_specs=in_specs,
        out_specs=tok,
        out_shape=jax.ShapeDtypeStruct((nb, nt, d), F32),
        scratch_shapes=scratch,
        compiler_params=pltpu.CompilerParams(dimension_semantics=("parallel", "arbitrary"),
                                             vmem_limit_bytes=VMEM_LIMIT),
        name="layer",
    )(*operands)


def _block_diag(w):
    depth, g, n, _ = w.shape
    eye = jnp.eye(g, dtype=w.dtype)
    return (w[:, :, :, None, :] * eye[None, :, None, :, None]).reshape(depth, g * n, g * n)


def kernel(x, c, norm_g, w_mod, b_mod, w_in, rwkv_mu, rwkv_w0, rwkv_w2, rwkv_a0, rwkv_a2, rwkv_kk, rwkv_ka, rwkv_rk, rwkv_ln_g, rwkv_ln_b, hgrn_lb, hgrn_norm_g, ret_norm_g, lru_conv_w, lru_conv_b, lru_wa, lru_ba, lru_wx, lru_bx, lru_lam, w_branch, w_out, final_g):
    nb, nt, d = x.shape
    depth = w_in.shape[0]
    tm = min(TOKEN_TILE, nt)
    assert nt % tm == 0 and tm % (2 * CHUNK) == 0 and CHUNK == HEAD
    assert w_in.shape[2] == MIX_W + N_BRANCH * d

    lb_p = jax.nn.softmax(hgrn_lb.astype(F32), axis=0)
    params = dict(
        norm_g=norm_g, w_in=w_in.astype(BF16),
        rwkv=dict(mu=rwkv_mu, w0=rwkv_w0, w2=rwkv_w2.astype(BF16), a0=rwkv_a0,
                  a2=rwkv_a2.astype(BF16), kk=rwkv_kk, ka=rwkv_ka, rk=rwkv_rk,
                  ln_g=rwkv_ln_g, ln_b=rwkv_ln_b),
        hgrn_lb=jnp.cumsum(lb_p, axis=0) - lb_p[0], hgrn_g=hgrn_norm_g, ret_g=ret_norm_g,
        lru=dict(conv_w=lru_conv_w, conv_b=lru_conv_b, wa=_block_diag(lru_wa).astype(BF16),
                 ba=lru_ba, wx=_block_diag(lru_wx).astype(BF16), bx=lru_bx, lam=lru_lam),
        w_branch=w_branch.astype(BF16), w_out=w_out.astype(BF16), final_g=final_g.reshape(1, d))

    head_id = np.arange(BW) // HEAD
    tok = np.arange(tm)
    consts = dict(
        hsum=jnp.asarray((head_id[:, None] == head_id[None, :]).astype(np.float32), dtype=BF16),
        tril=jnp.asarray(((tok[:, None] // CHUNK == tok[None, :] // CHUNK)
                          & (tok[None, :] <= tok[:, None])).astype(np.float32), dtype=BF16),
        ret=dict(zip(_RET_KEYS, _ret_tables(nt))))

    mod = _modulation(c, w_mod, b_mod).reshape(depth, nb, 3, d)
    h = x
    for l in range(depth):
        h = _layer(l, h, mod, params, consts, final_norm=(l == depth - 1))
    return h
```

```python
import functools

import numpy as np
import jax
import jax.numpy as jnp
from jax import lax
from jax.experimental import pallas as pl
from jax.experimental.pallas import tpu as pltpu

F32 = jnp.float32
BF16 = jnp.bfloat16

N_BRANCH = 4
BW = 256
N_HEAD = 4
HEAD = BW // N_HEAD
LANE = 128
EPS = 1e-6
RWKV_LORA = 64
RWKV_SHIFT = 3 * BW + 2 * RWKV_LORA
RWKV_W = RWKV_SHIFT + BW
RWKV_LN_EPS = 64e-5
HGRN_W = 4 * BW
RET_QKHEAD = HEAD // 2
RET_QK = N_HEAD * RET_QKHEAD
RET_W = 2 * RET_QK + 2 * BW
ROPE_BASE = 10000.0
LRU_W = 2 * BW
CONV_WIDTH = 4
LRU_C = 8.0
MIX_W = RWKV_W + HGRN_W + RET_W + LRU_W

CHUNK = 64
N_LEVEL = 6
TOKEN_TILE = 128
SEQ_PER_STEP = 4
FEED_PIECE = 256
SEQ_LAG = 2
VMEM_LIMIT = 56 * 1024 * 1024


def _dot(a, b):
    return jnp.dot(a.astype(BF16), b.astype(BF16), preferred_element_type=F32)


def _dot_nt(a, b):
    return lax.dot_general(a.astype(BF16), b.astype(BF16), (((1,), (1,)), ((), ())),
                           preferred_element_type=F32)


def _dot_tn(a, b):
    return lax.dot_general(a.astype(BF16), b.astype(BF16), (((0,), (0,)), ((), ())),
                           preferred_element_type=F32)


def _split2(x):
    hi = x.astype(BF16)
    lo = (x - hi.astype(F32)).astype(BF16)
    return hi, lo


def _dot_hi(a, b):
    ah, al = _split2(a)
    bh, bl = _split2(b)
    d = functools.partial(jnp.dot, preferred_element_type=F32)
    return d(ah, bh) + (d(ah, bl) + d(al, bh))


def _dot01_left(m01, x):
    d = functools.partial(jnp.dot, preferred_element_type=F32)
    hi, lo = _split2(x)
    return d(m01, hi) + d(m01, lo)


def _head_sum(x, m01):
    return jnp.dot(x.astype(BF16), m01, preferred_element_type=F32)


def _sigmoid(x):
    return 0.5 * jnp.tanh(0.5 * x) + 0.5


def _silu(x):
    return x * _sigmoid(x)


def _softplus(x):
    return jnp.maximum(x, 0.0) + jnp.log1p(jnp.exp(-jnp.abs(x)))


def _adaln(x, g, mod):
    ms = jnp.mean(x * x, axis=-1, keepdims=True)
    y = x * lax.rsqrt(ms + EPS) * g
    return y * (1.0 + mod[1:2, :]) + mod[0:1, :]


def _shift_rows(x, k, tail):
    xr = pltpu.roll(x, k, axis=0)
    row = lax.broadcasted_iota(jnp.int32, (8, x.shape[1]), 0)
    head = jnp.where(row < k, pltpu.roll(tail, k, axis=0), xr[0:8, :])
    return jnp.concatenate([head, xr[8:, :]], axis=0)


def _round_robin(gens):
    gens = list(gens)
    results = [None] * len(gens)
    alive = list(range(len(gens)))
    while alive:
        still = []
        for idx in alive:
            try:
                next(gens[idx])
                still.append(idx)
            except StopIteration as stop:
                results[idx] = stop.value
        alive = still
        yield
    return results


def _run(gen):
    try:
        while True:
            next(gen)
    except StopIteration as stop:
        return stop.value


def _mod_kernel(c_ref, w_ref, b_ref, o_ref):
    o_ref[...] = _dot_hi(_silu(c_ref[...]), w_ref[...]) + b_ref[...]


def _modulation(c, w_mod, b_mod):
    depth, d, d3 = w_mod.shape
    nb = c.shape[0]
    n_col = d3 // d
    return pl.pallas_call(
        _mod_kernel,
        grid=(depth, n_col),
        in_specs=[pl.BlockSpec((nb, d), lambda l, j: (0, 0)),
                  pl.BlockSpec((None, d, d), lambda l, j: (l, 0, j)),
                  pl.BlockSpec((None, 1, d), lambda l, j: (l, 0, j))],
        out_specs=pl.BlockSpec((None, nb, d), lambda l, j: (l, 0, j)),
        out_shape=jax.ShapeDtypeStruct((depth, nb, d3), F32),
        name="modulation",
    )(c, w_mod, b_mod.reshape(depth, 1, d3))


def _head_lane_masks(bd01):
    return [bd01[i * HEAD:i * HEAD + CHUNK, 0:LANE] for i in range(LANE // HEAD)]


def _block(x, masks):
    xb = x.astype(BF16)
    zero = jnp.zeros((x.shape[0], LANE), BF16)
    per = LANE // HEAD
    rows = []
    for h in range(N_HEAD):
        g = h // per
        part = xb[:, g * LANE:(g + 1) * LANE] * masks[h % per]
        rows.append(jnp.concatenate([part if i == g else zero for i in range(BW // LANE)], axis=1))
    return jnp.concatenate(rows, axis=0)


def _rows_to_wide(xt, c0):
    return jnp.concatenate([xt[h * HEAD:(h + 1) * HEAD, c0:c0 + CHUNK] for h in range(N_HEAD)],
                           axis=1)


def _unit_lower_inverses(a_list, eye_wide, hmask):
    n = len(a_list)
    xs = [_dot(a, _block(a, hmask)) for a in a_list]
    invs = [eye_wide + a for a in a_list]
    for lvl in range(1, N_LEVEL):
        yield
        for i in range(n):
            xb = _block(xs[i], hmask)
            if lvl < N_LEVEL - 1:
                res = _dot(jnp.concatenate([xs[i], invs[i]], axis=0), xb)
                xs[i] = res[:CHUNK]
                invs[i] = invs[i] + res[CHUNK:]
            else:
                invs[i] = invs[i] + _dot(invs[i], xb)
    return invs


def _rwkv_seq(u, p, tril_ref, hsum_ref, tail_ref, s_ref):
    tm = u.shape[0]
    feats = u[:, :RWKV_SHIFT]
    gate = u[:, RWKV_SHIFT:]
    prev = _shift_rows(feats, 1, tail_ref[...])
    tail_ref[...] = feats[tm - 8:, :]
    x = feats + p["mu"][...] * (prev - feats)
    r = x[:, 0:BW]
    k = x[:, BW:2 * BW]
    v = x[:, 2 * BW:3 * BW]
    w_lo = x[:, 3 * BW:3 * BW + RWKV_LORA]
    a_lo = x[:, 3 * BW + RWKV_LORA:]

    hsum = hsum_ref[...]
    log_w = -float(np.exp(-0.5)) * _sigmoid(p["w0"][...] + _dot(jnp.tanh(w_lo), p["w2"][...]))
    a = _sigmoid(p["a0"][...] + _dot(a_lo, p["a2"][...]))
    kk = k * p["kk"][...]
    k = k * (1.0 + (a - 1.0) * p["ka"][...])
    kk = kk / jnp.maximum(jnp.sqrt(_head_sum(kk * kk, hsum)), 1e-12)
    beta = kk * a

    c = _dot01_left(tril_ref[...], log_w)
    n_chunk = tm // CHUNK
    hmask = _head_lane_masks(hsum)
    row = lax.broadcasted_iota(jnp.int32, (CHUNK, BW), 0)
    pos = lax.broadcasted_iota(jnp.int32, (CHUNK, BW), 1) & (HEAD - 1)
    strict = row > pos
    lower = row >= pos
    eye_wide = (row == pos).astype(F32)

    al_t, r_t, a_ab, a_ak, a_rb, a_rk, b_eT, kv_c, e_col, v_blk = ([] for _ in range(10))
    yield
    for j in range(n_chunk):
        sl = slice(j * CHUNK, (j + 1) * CHUNK)
        cj = c[sl]
        c_last = cj[CHUNK - 1:CHUNK, :]
        e_in = jnp.exp(cj)
        e_prev = jnp.exp(cj - log_w[sl])
        e_out = jnp.exp(-cj)
        e_rest = jnp.exp(c_last) * e_out
        al_t.append(-kk[sl] * e_prev)
        r_t.append(r[sl] * e_in)
        rhs = jnp.concatenate([_block(beta[sl] * e_out, hmask), _block(k[sl] * e_out, hmask)], axis=0)
        pm = _dot_nt(jnp.concatenate([al_t[j], r_t[j]], axis=0), rhs)
        a_ab.append(jnp.where(strict, pm[:CHUNK, :BW], 0.0))
        a_ak.append(jnp.where(strict, pm[:CHUNK, BW:], 0.0))
        a_rb.append(jnp.where(lower, pm[CHUNK:, :BW], 0.0))
        a_rk.append(jnp.where(lower, pm[CHUNK:, BW:], 0.0))
        ends = jnp.concatenate([beta[sl] * e_rest, k[sl] * e_rest], axis=0).T
        b_eT.append(_rows_to_wide(ends, 0))
        v_blk.append(_block(v[sl], hmask))
        kv_c.append(_dot(_rows_to_wide(ends, CHUNK), v_blk[j]))
        col = jnp.exp(jnp.broadcast_to(c_last, (2 * CHUNK, BW)).T)
        e_col.append(_rows_to_wide(col, 0))
        yield
    avs = [_dot(jnp.concatenate([a_ak[j], a_rk[j]], axis=0), v_blk[j]) for j in range(n_chunk)]
    invs = yield from _unit_lower_inverses(a_ab, eye_wide, hmask)
    yield
    tws = [_dot(invs[j], jnp.concatenate([_block(al_t[j], hmask), _block(avs[j][:CHUNK], hmask)],
                                         axis=1)) for j in range(n_chunk)]
    t_al = [tw[:, :BW] for tw in tws]
    t_akv = [tw[:, BW:] for tw in tws]
    yield
    qzs = [_dot(a_rb[j], jnp.concatenate([_block(t_al[j], hmask), _block(t_akv[j], hmask)], axis=1))
           for j in range(n_chunk)]
    q_c = [r_t[j] + qzs[j][:, :BW] for j in range(n_chunk)]
    z_c = [qzs[j][:, BW:] + avs[j][CHUNK:] for j in range(n_chunk)]
    yield

    s = s_ref[...]
    ys = []
    for j in range(n_chunk):
        uo = _dot(jnp.concatenate([t_al[j], q_c[j]], axis=0), _block(s, hmask))
        ys.append(uo[CHUNK:] + z_c[j])
        uu = uo[:CHUNK] + t_akv[j]
        yield
        s = e_col[j] * s + _dot(b_eT[j], _block(uu, hmask)) + kv_c[j]
        yield
    s_ref[...] = s
    y = jnp.concatenate(ys, axis=0)

    mean = _head_sum(y, hsum) * (1.0 / HEAD)
    yc = y - mean
    var = _head_sum(yc * yc, hsum) * (1.0 / HEAD)
    y = yc * lax.rsqrt(var + RWKV_LN_EPS) * p["ln_g"][...] + p["ln_b"][...]
    bonus = _head_sum(r * k * p["rk"][...], hsum) * v
    return (y + bonus) * _silu(gate)


def _midpoint_rows(b, lvl, row):
    blk = CHUNK >> lvl
    half = blk // 2
    if blk >= 8:
        return jnp.concatenate(
            [jnp.broadcast_to(b[s + half - 1:s + half, :], (blk, b.shape[1]))
             for s in range(0, CHUNK, blk)], axis=0)
    p = row & (blk - 1)
    out = b
    for d in range(-(half - 1), half + 1):
        if d != 0:
            out = jnp.where(p == half - 1 + d, pltpu.roll(b, d % CHUNK, axis=0), out)
    return out


def _hgrn_seq(u, lb_ref, g_ref, tril_ref, hsum_ref, s_ref):
    tm = u.shape[0]
    q = u[:, 0:BW]
    kx = (1.0 - lb_ref[...]) * _sigmoid(-u[:, BW:2 * BW])
    log_f = jnp.log(1.0 - kx)
    v = u[:, 2 * BW:3 * BW]
    gate = u[:, 3 * BW:]
    b_all = _dot01_left(tril_ref[...], log_f)

    hmask = _head_lane_masks(hsum_ref[...])
    row = lax.broadcasted_iota(jnp.int32, (CHUNK, BW), 0)
    pos = lax.broadcasted_iota(jnp.int32, (CHUNK, BW), 1) & (HEAD - 1)
    masks = []
    for lvl in range(N_LEVEL):
        sh = N_LEVEL - lvl
        half = 1 << (sh - 1)
        same = (row >> sh) == (pos >> sh)
        masks.append(same & ((row & half) != 0) & ((pos & half) == 0))
    diag = row == pos

    yield
    parts = []
    for j in range(0, tm // CHUNK, 2):
        pair = []
        for jj in (j, j + 1):
            sl = slice(jj * CHUNK, (jj + 1) * CHUNK)
            qj, kj, vj, b = q[sl], kx[sl], v[sl], b_all[sl]
            b_last = b[CHUNK - 1:CHUNK, :]
            qb, kb = qj.astype(BF16), kj.astype(BF16)
            sc = jnp.where(diag, _dot_nt(qb, _block(kb, hmask)), 0.0)
            for lvl in range(N_LEVEL):
                e = jnp.exp(-jnp.abs(b - _midpoint_rows(b, lvl, row))).astype(BF16)
                sc = jnp.where(masks[lvl], _dot_nt(qb * e, _block(kb * e, hmask)), sc)
            col = jnp.exp(jnp.broadcast_to(b_last, (2 * CHUNK, BW)).T)
            pair.append((sc, qj * jnp.exp(b), _block(vj, hmask), kj * jnp.exp(b_last - b),
                         _rows_to_wide(col, 0)))
            yield
        k_eT = jnp.concatenate([pair[0][3], pair[1][3]], axis=0).T
        for i, (sc, q_in, v_blk, _, e_col) in enumerate(pair):
            kv = _dot(_rows_to_wide(k_eT, i * CHUNK), v_blk)
            parts.append((jnp.concatenate([sc, q_in], axis=1), v_blk, e_col, kv))
        yield
    s = s_ref[...]
    ys = []
    for lhs, v_blk, e_col, kv in parts:
        ys.append(_dot(lhs, jnp.concatenate([v_blk, _block(s, hmask)], axis=0)))
        s = e_col * s + kv
        yield
    s_ref[...] = s
    y = jnp.concatenate(ys, axis=0)
    ms = _head_sum(y * y, hsum_ref[...]) * (1.0 / HEAD)
    return y * lax.rsqrt(ms + EPS) * g_ref[...] * _silu(gate)


def _ret_tables(nt):
    dk, c = RET_QKHEAD, CHUNK
    pos = jnp.arange(nt, dtype=F32)
    inv_freq = 1.0 / (ROPE_BASE ** jnp.linspace(0.0, 1.0, dk // 2, dtype=F32))
    ang = pos[:, None] * inv_freq[None, :]
    cos, sin = jnp.cos(ang), jnp.sin(ang)
    cos_t = jnp.tile(jnp.concatenate([cos, cos], -1), (1, N_HEAD))
    sin_t = jnp.tile(jnp.concatenate([-sin, sin], -1), (1, N_HEAD))
    log_gamma = jnp.log1p(-jnp.exp2(-5.0 - jnp.arange(N_HEAD, dtype=F32)))
    idx = jnp.arange(c, dtype=F32)
    rel = idx[:, None] - idx[None, :]
    decay = jnp.where(rel >= 0, jnp.exp(log_gamma[:, None, None] * jnp.maximum(rel, 0.0)), 0.0)
    decay_wide = jnp.concatenate([decay[h] for h in range(N_HEAD)], axis=1)
    zeta = jnp.exp(log_gamma[:, None] * (c - 1 - idx))
    xi = jnp.exp(log_gamma[:, None] * (idx + 1.0))
    gamma_c = jnp.exp(log_gamma * c)
    lanes = lambda t: jnp.repeat(t.T, dk, axis=1)
    gam = jnp.broadcast_to(jnp.repeat(gamma_c, dk)[:, None], (RET_QK, BW))
    key_head = np.arange(RET_QK) // dk
    kmask = jnp.asarray((np.repeat(np.arange(N_HEAD), c)[:, None] == key_head[None, :])
                        .astype(np.float32), dtype=BF16)
    smask = jnp.asarray((key_head[:, None] == (np.arange(BW) // HEAD)[None, :]).astype(np.float32))
    return cos_t, sin_t, decay_wide, lanes(zeta), lanes(xi), gam, kmask, smask


def _ret_seq(u, cos_ref, sin_ref, dec_ref, zeta_ref, xi_ref, gam_ref, kmask_ref, smask_ref, g_ref,
             hsum_ref, s_ref):
    tm = u.shape[0]
    cos, sin = cos_ref[...], sin_ref[...]
    lane = lax.broadcasted_iota(jnp.int32, cos.shape, 1)
    first_half = (lane & (RET_QKHEAD // 2)) == 0

    def rope(t):
        half = RET_QKHEAD // 2
        swapped = jnp.where(first_half, pltpu.roll(t, RET_QK - half, axis=1),
                            pltpu.roll(t, half, axis=1))
        return t * cos + swapped * sin

    q = rope(u[:, 0:RET_QK])
    k = rope(u[:, RET_QK:2 * RET_QK]) * (RET_QKHEAD ** -0.5)
    v = u[:, 2 * RET_QK:2 * RET_QK + BW]
    gate = u[:, 2 * RET_QK + BW:]
    hmask = _head_lane_masks(hsum_ref[...])
    dec, xi, zeta = dec_ref[...], xi_ref[...], zeta_ref[...]
    kmask, smask, gam = kmask_ref[...], smask_ref[...], gam_ref[...]
    parts = []
    for j in range(tm // CHUNK):
        yield
        sl = slice(j * CHUNK, (j + 1) * CHUNK)
        qj, kj, vj = q[sl], k[sl], v[sl]
        k_blk = jnp.concatenate([kj.astype(BF16)] * N_HEAD, axis=0) * kmask
        sc = _dot_nt(qj, k_blk) * dec
        kv = smask * _dot_tn(kj * zeta, vj)
        parts.append((jnp.concatenate([sc, qj * xi], axis=1), _block(vj, hmask), kv))
    yield
    s = s_ref[...]
    ys = []
    for lhs, v_blk, kv in parts:
        ys.append(_dot(lhs, jnp.concatenate([v_blk, s.astype(BF16)], axis=0)))
        s = gam * s + kv
    s_ref[...] = s
    y = jnp.concatenate(ys, axis=0)
    ms = _head_sum(y * y, hsum_ref[...]) * (1.0 / HEAD)
    return y * lax.rsqrt(ms + EPS) * g_ref[...] * _silu(gate)


def _lru_seq(u, p, tail_ref, h_ref):
    tm = u.shape[0]
    x = u[:, :BW]
    gate = u[:, BW:]
    tail = tail_ref[...]
    cw = p["conv_w"][...]
    y = p["conv_b"][...] + cw[CONV_WIDTH - 1:CONV_WIDTH, :] * x
    for k in range(1, CONV_WIDTH):
        y = y + cw[CONV_WIDTH - 1 - k:CONV_WIDTH - k, :] * _shift_rows(x, k, tail)
    tail_ref[...] = x[tm - 8:, :]
    yield

    r = _sigmoid(_dot(y, p["wa"][...]) + p["ba"][...])
    i = _sigmoid(_dot(y, p["wx"][...]) + p["bx"][...])
    log_a = -LRU_C * r * _softplus(-p["lam"][...])
    a = jnp.exp(log_a)
    mult = jnp.sqrt(jnp.tanh(-log_a) * (a * a + 1.0))
    row = lax.broadcasted_iota(jnp.int32, x.shape, 0)
    mult = jnp.where(row + pl.program_id(1) * tm == 0, 1.0, mult)
    b = mult * (i * y)

    d = 1
    while d < tm:
        yield
        if d % 8:
            a_sh = jnp.where(row < d, 1.0, pltpu.roll(a, d, axis=0))
            b_sh = jnp.where(row < d, 0.0, pltpu.roll(b, d, axis=0))
            b = a * b_sh + b
            a = a * a_sh
        else:
            b = jnp.concatenate([b[:d], a[d:] * b[:tm - d] + b[d:]], axis=0)
            a = jnp.concatenate([a[:d], a[d:] * a[:tm - d]], axis=0)
        d *= 2
    h = a * h_ref[0:1, :] + b
    h_ref[0:1, :] = h[tm - 1:tm, :]
    return h * _silu(gate)


_RWKV_KEYS = ("mu", "w0", "w2", "a0", "a2", "kk", "ka", "rk", "ln_g", "ln_b")
_LRU_KEYS = ("conv_w", "conv_b", "wa", "ba", "wx", "bx", "lam")
_RET_KEYS = ("cos", "sin", "decay", "zeta", "xi", "gam", "kmask", "smask")
_MATRIX_KEYS = ("w2", "a2", "conv_w", "wa", "wx")
N_SCRATCH = 6


def _then_project(mixer, wb_ref, branch):
    y = yield from mixer
    yield
    return _dot(y, wb_ref[branch]).astype(BF16)


def _when_ready(inputs, key, make, lag):
    while key not in inputs:
        yield
    for _ in range(lag):
        yield
    return (yield from make(inputs[key]))


def _feed_all(hns, w_in_ref, d, inputs):
    tm = hns[0].shape[0]
    hn = jnp.concatenate(hns, axis=0)
    per_seq = lambda a: [a[i * tm:(i + 1) * tm] for i in range(len(hns))]
    o = 0
    for key, n in enumerate((RWKV_W, HGRN_W, RET_W, LRU_W)):
        pieces = []
        for c0 in range(0, n, FEED_PIECE):
            w = min(FEED_PIECE, n - c0)
            pieces.append(jnp.dot(hn, w_in_ref[:, o + c0:o + c0 + w], preferred_element_type=F32))
            yield
        for i, u in enumerate(per_seq(jnp.concatenate(pieces, axis=1))):
            inputs[i][key] = u
        o += n
    gates = [[] for _ in hns]
    for bi in range(N_BRANCH):
        pieces = []
        for c0 in range(0, d, FEED_PIECE):
            col = MIX_W + bi * d + c0
            pieces.append(_sigmoid(jnp.dot(hn, w_in_ref[:, col:col + FEED_PIECE],
                                           preferred_element_type=F32).astype(BF16)))
            yield
        for i, g in enumerate(per_seq(jnp.concatenate(pieces, axis=1))):
            gates[i].append(g)
    return gates


def _layer_kernel(*refs, layer, final_norm):
    it = iter(refs)
    take = lambda n: [next(it) for _ in range(n)]
    row = lambda ref: ref.at[layer:layer + 1]
    x_ref, mod_ref, g_ref, w_in_ref = take(4)
    rwkv_p = dict(zip(_RWKV_KEYS, take(len(_RWKV_KEYS))))
    tril_ref, hsum_ref, lb_ref, hg_ref = take(4)
    ret_t = take(len(_RET_KEYS))
    (rg_ref,) = take(1)
    lru_p = dict(zip(_LRU_KEYS, take(len(_LRU_KEYS))))
    wb_ref, wo_ref, fg_ref, o_ref = take(4)
    scratch = take(N_SCRATCH)
    g_ref, lb_ref, hg_ref, rg_ref = row(g_ref), row(lb_ref), row(hg_ref), row(rg_ref)
    rwkv_p = {k: v if k in _MATRIX_KEYS else row(v) for k, v in rwkv_p.items()}
    lru_p = {k: v if k in _MATRIX_KEYS else row(v) for k, v in lru_p.items()}

    @pl.when(pl.program_id(1) == 0)
    def _():
        for ref in scratch:
            ref[...] = jnp.zeros_like(ref)

    ns, _, d = x_ref.shape
    xs = [x_ref[i] for i in range(ns)]
    mods = [mod_ref[i] for i in range(ns)]
    hns = [_adaln(xs[i], g_ref[...], mods[i]).astype(BF16) for i in range(ns)]

    def mixers(i):
        rw_tail, rw_s, hg_s, rt_s, lr_tail, lr_h = (ref.at[i] for ref in scratch)
        return (lambda u: _rwkv_seq(u, rwkv_p, tril_ref, hsum_ref, rw_tail, rw_s),
                lambda u: _hgrn_seq(u, lb_ref, hg_ref, tril_ref, hsum_ref, hg_s),
                lambda u: _ret_seq(u, *ret_t, rg_ref, hsum_ref, rt_s),
                lambda u: _lru_seq(u, lru_p, lr_tail, lr_h))

    inputs = [{} for _ in range(ns)]
    gates, *projs = _run(_round_robin(
        [_feed_all(hns, w_in_ref, d, inputs)]
        + [_then_project(_when_ready(inputs[i], bi, mixers(i)[bi], i * SEQ_LAG), wb_ref, bi)
           for bi in range(N_BRANCH) for i in range(ns)]))
    merged = []
    for i in range(ns):
        terms = [gates[i][bi] * projs[bi * ns + i] for bi in range(N_BRANCH)]
        merged.append((terms[0] + terms[1]) + (terms[2] + terms[3]))
    update = _dot(jnp.concatenate(merged, axis=0), wo_ref[...])
    tm = xs[0].shape[0]
    for i in range(ns):
        out = xs[i] + mods[i][2:3, :] * update[i * tm:(i + 1) * tm]
        if final_norm:
            out = out * lax.rsqrt(jnp.mean(out * out, axis=-1, keepdims=True) + EPS) * fg_ref[...]
        o_ref[i] = out


def _resident(shape, index_map):
    return pl.BlockSpec(shape, index_map, pipeline_mode=pl.Buffered(1))


def _layer(layer, h, mod, p, consts, final_norm):
    nb, nt, d = h.shape
    tm = min(TOKEN_TILE, nt)
    ns = SEQ_PER_STEP if nb % SEQ_PER_STEP == 0 else 1
    const = lambda a: _resident(a.shape, lambda b, t, nd=a.ndim: (0,) * nd)
    per_layer = lambda a: const(a) if a.ndim == 2 else _resident(
        (None,) + a.shape[1:], lambda b, t, nd=a.ndim: (layer,) + (0,) * (nd - 1))
    tok = pl.BlockSpec((ns, tm, d), lambda b, t: (b, t, 0))
    pos = pl.BlockSpec((tm, RET_QK), lambda b, t: (t, 0))
    ret = consts["ret"]
    operands = ([h, mod, p["norm_g"], p["w_in"]] + [p["rwkv"][k] for k in _RWKV_KEYS]
                + [consts["tril"], consts["hsum"], p["hgrn_lb"], p["hgrn_g"]]
                + [ret[k] for k in _RET_KEYS] + [p["ret_g"]] + [p["lru"][k] for k in _LRU_KEYS]
                + [p["w_branch"], p["w_out"], p["final_g"]])
    in_specs = ([tok, pl.BlockSpec((None, ns, 3, d), lambda b, t: (layer, b, 0, 0)),
                 per_layer(p["norm_g"]), per_layer(p["w_in"])]
                + [per_layer(p["rwkv"][k]) for k in _RWKV_KEYS]
                + [const(consts["tril"]), const(consts["hsum"]), per_layer(p["hgrn_lb"]),
                   per_layer(p["hgrn_g"]), pos, pos]
                + [const(ret[k]) for k in _RET_KEYS[2:]] + [per_layer(p["ret_g"])]
                + [per_layer(p["lru"][k]) for k in _LRU_KEYS]
                + [per_layer(p["w_branch"]), per_layer(p["w_out"]), const(p["final_g"])])
    scratch = [pltpu.VMEM((ns, 8, RWKV_SHIFT), F32), pltpu.VMEM((ns, HEAD, BW), F32),
               pltpu.VMEM((ns, HEAD, BW), F32), pltpu.VMEM((ns, RET_QK, BW), F32),
               pltpu.VMEM((ns, 8, BW), F32), pltpu.VMEM((ns, 8, BW), F32)]
    assert len(scratch) == N_SCRATCH
    return pl.pallas_call(
        functools.partial(_layer_kernel, layer=layer, final_norm=final_norm),
        grid=(nb // ns, nt // tm),
        in_specs=in_specs,
        out_specs=tok,
        out_shape=jax.ShapeDtypeStruct((nb, nt, d), F32),
        scratch_shapes=scratch,
        compiler_params=pltpu.CompilerParams(dimension_semantics=("parallel", "arbitrary"),
                                             vmem_limit_bytes=VMEM_LIMIT),
        name="layer",
    )(*operands)


def _block_diag(w):
    depth, g, n, _ = w.shape
    eye = jnp.eye(g, dtype=w.dtype)
    return (w[:, :, :, None, :] * eye[None, :, None, :, None]).reshape(depth, g * n, g * n)


def kernel(x, c, norm_g, w_mod, b_mod, w_in, rwkv_mu, rwkv_w0, rwkv_w2, rwkv_a0, rwkv_a2, rwkv_kk, rwkv_ka, rwkv_rk, rwkv_ln_g, rwkv_ln_b, hgrn_lb, hgrn_norm_g, ret_norm_g, lru_conv_w, lru_conv_b, lru_wa, lru_ba, lru_wx, lru_bx, lru_lam, w_branch, w_out, final_g):
    nb, nt, d = x.shape
    depth = w_in.shape[0]
    tm = min(TOKEN_TILE, nt)
    assert nt % tm == 0 and tm % (2 * CHUNK) == 0 and CHUNK == HEAD
    assert w_in.shape[2] == MIX_W + N_BRANCH * d

    lb_p = jax.nn.softmax(hgrn_lb.astype(F32), axis=0)
    params = dict(
        norm_g=norm_g, w_in=w_in.astype(BF16),
        rwkv=dict(mu=rwkv_mu, w0=rwkv_w0, w2=rwkv_w2.astype(BF16), a0=rwkv_a0,
                  a2=rwkv_a2.astype(BF16), kk=rwkv_kk, ka=rwkv_ka, rk=rwkv_rk,
                  ln_g=rwkv_ln_g, ln_b=rwkv_ln_b),
        hgrn_lb=jnp.cumsum(lb_p, axis=0) - lb_p[0], hgrn_g=hgrn_norm_g, ret_g=ret_norm_g,
        lru=dict(conv_w=lru_conv_w, conv_b=lru_conv_b, wa=_block_diag(lru_wa).astype(BF16),
                 ba=lru_ba, wx=_block_diag(lru_wx).astype(BF16), bx=lru_bx, lam=lru_lam),
        w_branch=w_branch.astype(BF16), w_out=w_out.astype(BF16), final_g=final_g.reshape(1, d))

    head_id = np.arange(BW) // HEAD
    tok = np.arange(tm)
    consts = dict(
        hsum=jnp.asarray((head_id[:, None] == head_id[None, :]).astype(np.float32), dtype=BF16),
        tril=jnp.asarray(((tok[:, None] // CHUNK == tok[None, :] // CHUNK)
                          & (tok[None, :] <= tok[:, None])).astype(np.float32), dtype=BF16),
        ret=dict(zip(_RET_KEYS, _ret_tables(nt))))

    mod = _modulation(c, w_mod, b_mod).reshape(depth, nb, 3, d)
    h = x
    for l in range(depth):
        h = _layer(l, h, mod, params, consts, final_norm=(l == depth - 1))
    return h
```

```python
import functools

import numpy as np
import jax
import jax.numpy as jnp
from jax import lax
from jax.experimental import pallas as pl
from jax.experimental.pallas import tpu as pltpu

F32 = jnp.float32
BF16 = jnp.bfloat16

N_BRANCH = 4
BW = 256
N_HEAD = 4
HEAD = BW // N_HEAD
LANE = 128
EPS = 1e-6
RWKV_LORA = 64
RWKV_SHIFT = 3 * BW + 2 * RWKV_LORA
RWKV_W = RWKV_SHIFT + BW
RWKV_LN_EPS = 64e-5
HGRN_W = 4 * BW
RET_QKHEAD = HEAD // 2
RET_QK = N_HEAD * RET_QKHEAD
RET_W = 2 * RET_QK + 2 * BW
ROPE_BASE = 10000.0
LRU_W = 2 * BW
CONV_WIDTH = 4
LRU_C = 8.0
MIX_W = RWKV_W + HGRN_W + RET_W + LRU_W

CHUNK = 64
N_LEVEL = 6
TOKEN_TILE = 128
SEQ_PER_STEP = 4
FEED_PIECE = 512
SEQ_LAG = 2
VMEM_LIMIT = 56 * 1024 * 1024


def _dot(a, b):
    return jnp.dot(a.astype(BF16), b.astype(BF16), preferred_element_type=F32)


def _dot_nt(a, b):
    return lax.dot_general(a.astype(BF16), b.astype(BF16), (((1,), (1,)), ((), ())),
                           preferred_element_type=F32)


def _dot_tn(a, b):
    return lax.dot_general(a.astype(BF16), b.astype(BF16), (((0,), (0,)), ((), ())),
                           preferred_element_type=F32)


def _split2(x):
    hi = x.astype(BF16)
    lo = (x - hi.astype(F32)).astype(BF16)
    return hi, lo


def _dot_hi(a, b):
    ah, al = _split2(a)
    bh, bl = _split2(b)
    d = functools.partial(jnp.dot, preferred_element_type=F32)
    return d(ah, bh) + (d(ah, bl) + d(al, bh))


def _dot01_left(m01, x):
    d = functools.partial(jnp.dot, preferred_element_type=F32)
    hi, lo = _split2(x)
    return d(m01, hi) + d(m01, lo)


def _head_sum(x, m01):
    return jnp.dot(x.astype(BF16), m01, preferred_element_type=F32)


def _sigmoid(x):
    return 0.5 * jnp.tanh(0.5 * x) + 0.5


def _silu(x):
    return x * _sigmoid(x)


def _softplus(x):
    return jnp.maximum(x, 0.0) + jnp.log1p(jnp.exp(-jnp.abs(x)))


def _adaln(x, g, mod):
    ms = jnp.mean(x * x, axis=-1, keepdims=True)
    y = x * lax.rsqrt(ms + EPS) * g
    return y * (1.0 + mod[1:2, :]) + mod[0:1, :]


def _shift_rows(x, k, tail):
    xr = pltpu.roll(x, k, axis=0)
    row = lax.broadcasted_iota(jnp.int32, (8, x.shape[1]), 0)
    head = jnp.where(row < k, pltpu.roll(tail, k, axis=0), xr[0:8, :])
    return jnp.concatenate([head, xr[8:, :]], axis=0)


def _round_robin(gens):
    gens = list(gens)
    results = [None] * len(gens)
    alive = list(range(len(gens)))
    while alive:
        still = []
        for idx in alive:
            try:
                next(gens[idx])
                still.append(idx)
            except StopIteration as stop:
                results[idx] = stop.value
        alive = still
        yield
    return results


def _run(gen):
    try:
        while True:
            next(gen)
    except StopIteration as stop:
        return stop.value


def _mod_kernel(c_ref, w_ref, b_ref, o_ref):
    o_ref[...] = _dot_hi(_silu(c_ref[...]), w_ref[...]) + b_ref[...]


def _modulation(c, w_mod, b_mod):
    depth, d, d3 = w_mod.shape
    nb = c.shape[0]
    n_col = d3 // d
    return pl.pallas_call(
        _mod_kernel,
        grid=(depth, n_col),
        in_specs=[pl.BlockSpec((nb, d), lambda l, j: (0, 0)),
                  pl.BlockSpec((None, d, d), lambda l, j: (l, 0, j)),
                  pl.BlockSpec((None, 1, d), lambda l, j: (l, 0, j))],
        out_specs=pl.BlockSpec((None, nb, d), lambda l, j: (l, 0, j)),
        out_shape=jax.ShapeDtypeStruct((depth, nb, d3), F32),
        name="modulation",
    )(c, w_mod, b_mod.reshape(depth, 1, d3))


def _head_lane_masks(bd01):
    return [bd01[i * HEAD:i * HEAD + CHUNK, 0:LANE] for i in range(LANE // HEAD)]


def _block(x, masks):
    xb = x.astype(BF16)
    zero = jnp.zeros((x.shape[0], LANE), BF16)
    per = LANE // HEAD
    rows = []
    for h in range(N_HEAD):
        g = h // per
        part = xb[:, g * LANE:(g + 1) * LANE] * masks[h % per]
        rows.append(jnp.concatenate([part if i == g else zero for i in range(BW // LANE)], axis=1))
    return jnp.concatenate(rows, axis=0)


def _rows_to_wide(xt, c0):
    return jnp.concatenate([xt[h * HEAD:(h + 1) * HEAD, c0:c0 + CHUNK] for h in range(N_HEAD)],
                           axis=1)


def _unit_lower_inverses(a_list, eye_wide, hmask):
    n = len(a_list)
    xs = [_dot(a, _block(a, hmask)) for a in a_list]
    invs = [eye_wide + a for a in a_list]
    for lvl in range(1, N_LEVEL):
        yield
        for i in range(n):
            xb = _block(xs[i], hmask)
            if lvl < N_LEVEL - 1:
                res = _dot(jnp.concatenate([xs[i], invs[i]], axis=0), xb)
                xs[i] = res[:CHUNK]
                invs[i] = invs[i] + res[CHUNK:]
            else:
                invs[i] = invs[i] + _dot(invs[i], xb)
    return invs


def _rwkv_seq(u, p, tril_ref, hsum_ref, tail_ref, s_ref):
    tm = u.shape[0]
    feats = u[:, :RWKV_SHIFT]
    gate = u[:, RWKV_SHIFT:]
    prev = _shift_rows(feats, 1, tail_ref[...])
    tail_ref[...] = feats[tm - 8:, :]
    x = feats + p["mu"][...] * (prev - feats)
    r = x[:, 0:BW]
    k = x[:, BW:2 * BW]
    v = x[:, 2 * BW:3 * BW]
    w_lo = x[:, 3 * BW:3 * BW + RWKV_LORA]
    a_lo = x[:, 3 * BW + RWKV_LORA:]

    hsum = hsum_ref[...]
    log_w = -float(np.exp(-0.5)) * _sigmoid(p["w0"][...] + _dot(jnp.tanh(w_lo), p["w2"][...]))
    a = _sigmoid(p["a0"][...] + _dot(a_lo, p["a2"][...]))
    kk = k * p["kk"][...]
    k = k * (1.0 + (a - 1.0) * p["ka"][...])
    kk = kk / jnp.maximum(jnp.sqrt(_head_sum(kk * kk, hsum)), 1e-12)
    beta = kk * a

    c = _dot01_left(tril_ref[...], log_w)
    n_chunk = tm // CHUNK
    hmask = _head_lane_masks(hsum)
    row = lax.broadcasted_iota(jnp.int32, (CHUNK, BW), 0)
    pos = lax.broadcasted_iota(jnp.int32, (CHUNK, BW), 1) & (HEAD - 1)
    strict = row > pos
    lower = row >= pos
    eye_wide = (row == pos).astype(F32)

    al_t, r_t, a_ab, a_ak, a_rb, a_rk, b_eT, kv_c, e_col, v_blk = ([] for _ in range(10))
    yield
    for j in range(n_chunk):
        sl = slice(j * CHUNK, (j + 1) * CHUNK)
        cj = c[sl]
        c_last = cj[CHUNK - 1:CHUNK, :]
        e_in = jnp.exp(cj)
        e_prev = jnp.exp(cj - log_w[sl])
        e_out = jnp.exp(-cj)
        e_rest = jnp.exp(c_last) * e_out
        al_t.append(-kk[sl] * e_prev)
        r_t.append(r[sl] * e_in)
        rhs = jnp.concatenate([_block(beta[sl] * e_out, hmask), _block(k[sl] * e_out, hmask)], axis=0)
        pm = _dot_nt(jnp.concatenate([al_t[j], r_t[j]], axis=0), rhs)
        a_ab.append(jnp.where(strict, pm[:CHUNK, :BW], 0.0))
        a_ak.append(jnp.where(strict, pm[:CHUNK, BW:], 0.0))
        a_rb.append(jnp.where(lower, pm[CHUNK:, :BW], 0.0))
        a_rk.append(jnp.where(lower, pm[CHUNK:, BW:], 0.0))
        ends = jnp.concatenate([beta[sl] * e_rest, k[sl] * e_rest], axis=0).T
        b_eT.append(_rows_to_wide(ends, 0))
        v_blk.append(_block(v[sl], hmask))
        kv_c.append(_dot(_rows_to_wide(ends, CHUNK), v_blk[j]))
        col = jnp.exp(jnp.broadcast_to(c_last, (2 * CHUNK, BW)).T)
        e_col.append(_rows_to_wide(col, 0))
        yield
    avs = [_dot(jnp.concatenate([a_ak[j], a_rk[j]], axis=0), v_blk[j]) for j in range(n_chunk)]
    invs = yield from _unit_lower_inverses(a_ab, eye_wide, hmask)
    yield
    tws = [_dot(invs[j], jnp.concatenate([_block(al_t[j], hmask), _block(avs[j][:CHUNK], hmask)],
                                         axis=1)) for j in range(n_chunk)]
    t_al = [tw[:, :BW] for tw in tws]
    t_akv = [tw[:, BW:] for tw in tws]
    yield
    qzs = [_dot(a_rb[j], jnp.concatenate([_block(t_al[j], hmask), _block(t_akv[j], hmask)], axis=1))
           for j in range(n_chunk)]
    q_c = [r_t[j] + qzs[j][:, :BW] for j in range(n_chunk)]
    z_c = [qzs[j][:, BW:] + avs[j][CHUNK:] for j in range(n_chunk)]
    yield

    s = s_ref[...]
    ys = []
    for j in range(n_chunk):
        uo = _dot(jnp.concatenate([t_al[j], q_c[j]], axis=0), _block(s, hmask))
        ys.append(uo[CHUNK:] + z_c[j])
        uu = uo[:CHUNK] + t_akv[j]
        yield
        s = e_col[j] * s + _dot(b_eT[j], _block(uu, hmask)) + kv_c[j]
        yield
    s_ref[...] = s
    y = jnp.concatenate(ys, axis=0)

    mean = _head_sum(y, hsum) * (1.0 / HEAD)
    yc = y - mean
    var = _head_sum(yc * yc, hsum) * (1.0 / HEAD)
    y = yc * lax.rsqrt(var + RWKV_LN_EPS) * p["ln_g"][...] + p["ln_b"][...]
    bonus = _head_sum(r * k * p["rk"][...], hsum) * v
    return (y + bonus) * _silu(gate)


def _midpoint_rows(b, lvl, row):
    blk = CHUNK >> lvl
    half = blk // 2
    if blk >= 8:
        return jnp.concatenate(
            [jnp.broadcast_to(b[s + half - 1:s + half, :], (blk, b.shape[1]))
             for s in range(0, CHUNK, blk)], axis=0)
    p = row & (blk - 1)
    out = b
    for d in range(-(half - 1), half + 1):
        if d != 0:
            out = jnp.where(p == half - 1 + d, pltpu.roll(b, d % CHUNK, axis=0), out)
    return out


def _hgrn_seq(u, lb_ref, g_ref, tril_ref, hsum_ref, s_ref):
    tm = u.shape[0]
    q = u[:, 0:BW]
    kx = (1.0 - lb_ref[...]) * _sigmoid(-u[:, BW:2 * BW])
    log_f = jnp.log(1.0 - kx)
    v = u[:, 2 * BW:3 * BW]
    gate = u[:, 3 * BW:]
    b_all = _dot01_left(tril_ref[...], log_f)

    hmask = _head_lane_masks(hsum_ref[...])
    row = lax.broadcasted_iota(jnp.int32, (CHUNK, BW), 0)
    pos = lax.broadcasted_iota(jnp.int32, (CHUNK, BW), 1) & (HEAD - 1)
    masks = []
    for lvl in range(N_LEVEL):
        sh = N_LEVEL - lvl
        half = 1 << (sh - 1)
        same = (row >> sh) == (pos >> sh)
        masks.append(same & ((row & half) != 0) & ((pos & half) == 0))
    diag = row == pos

    yield
    parts = []
    for j in range(0, tm // CHUNK, 2):
        pair = []
        for jj in (j, j + 1):
            sl = slice(jj * CHUNK, (jj + 1) * CHUNK)
            qj, kj, vj, b = q[sl], kx[sl], v[sl], b_all[sl]
            b_last = b[CHUNK - 1:CHUNK, :]
            qb, kb = qj.astype(BF16), kj.astype(BF16)
            sc = jnp.where(diag, _dot_nt(qb, _block(kb, hmask)), 0.0)
            for lvl in range(N_LEVEL):
                e = jnp.exp(-jnp.abs(b - _midpoint_rows(b, lvl, row))).astype(BF16)
                sc = jnp.where(masks[lvl], _dot_nt(qb * e, _block(kb * e, hmask)), sc)
            col = jnp.exp(jnp.broadcast_to(b_last, (2 * CHUNK, BW)).T)
            pair.append((sc, qj * jnp.exp(b), _block(vj, hmask), kj * jnp.exp(b_last - b),
                         _rows_to_wide(col, 0)))
            yield
        k_eT = jnp.concatenate([pair[0][3], pair[1][3]], axis=0).T
        for i, (sc, q_in, v_blk, _, e_col) in enumerate(pair):
            kv = _dot(_rows_to_wide(k_eT, i * CHUNK), v_blk)
            parts.append((jnp.concatenate([sc, q_in], axis=1), v_blk, e_col, kv))
        yield
    s = s_ref[...]
    ys = []
    for lhs, v_blk, e_col, kv in parts:
        ys.append(_dot(lhs, jnp.concatenate([v_blk, _block(s, hmask)], axis=0)))
        s = e_col * s + kv
        yield
    s_ref[...] = s
    y = jnp.concatenate(ys, axis=0)
    ms = _head_sum(y * y, hsum_ref[...]) * (1.0 / HEAD)
    return y * lax.rsqrt(ms + EPS) * g_ref[...] * _silu(gate)


def _ret_tables(nt):
    dk, c = RET_QKHEAD, CHUNK
    pos = jnp.arange(nt, dtype=F32)
    inv_freq = 1.0 / (ROPE_BASE ** jnp.linspace(0.0, 1.0, dk // 2, dtype=F32))
    ang = pos[:, None] * inv_freq[None, :]
    cos, sin = jnp.cos(ang), jnp.sin(ang)
    cos_t = jnp.tile(jnp.concatenate([cos, cos], -1), (1, N_HEAD))
    sin_t = jnp.tile(jnp.concatenate([-sin, sin], -1), (1, N_HEAD))
    log_gamma = jnp.log1p(-jnp.exp2(-5.0 - jnp.arange(N_HEAD, dtype=F32)))
    idx = jnp.arange(c, dtype=F32)
    rel = idx[:, None] - idx[None, :]
    decay = jnp.where(rel >= 0, jnp.exp(log_gamma[:, None, None] * jnp.maximum(rel, 0.0)), 0.0)
    decay_wide = jnp.concatenate([decay[h] for h in range(N_HEAD)], axis=1)
    zeta = jnp.exp(log_gamma[:, None] * (c - 1 - idx))
    xi = jnp.exp(log_gamma[:, None] * (idx + 1.0))
    gamma_c = jnp.exp(log_gamma * c)
    lanes = lambda t: jnp.repeat(t.T, dk, axis=1)
    gam = jnp.broadcast_to(jnp.repeat(gamma_c, dk)[:, None], (RET_QK, BW))
    key_head = np.arange(RET_QK) // dk
    kmask = jnp.asarray((np.repeat(np.arange(N_HEAD), c)[:, None] == key_head[None, :])
                        .astype(np.float32), dtype=BF16)
    smask = jnp.asarray((key_head[:, None] == (np.arange(BW) // HEAD)[None, :]).astype(np.float32))
    return cos_t, sin_t, decay_wide, lanes(zeta), lanes(xi), gam, kmask, smask


def _ret_seq(u, cos_ref, sin_ref, dec_ref, zeta_ref, xi_ref, gam_ref, kmask_ref, smask_ref, g_ref,
             hsum_ref, s_ref):
    tm = u.shape[0]
    cos, sin = cos_ref[...], sin_ref[...]
    lane = lax.broadcasted_iota(jnp.int32, cos.shape, 1)
    first_half = (lane & (RET_QKHEAD // 2)) == 0

    def rope(t):
        half = RET_QKHEAD // 2
        swapped = jnp.where(first_half, pltpu.roll(t, RET_QK - half, axis=1),
                            pltpu.roll(t, half, axis=1))
        return t * cos + swapped * sin

    q = rope(u[:, 0:RET_QK])
    k = rope(u[:, RET_QK:2 * RET_QK]) * (RET_QKHEAD ** -0.5)
    v = u[:, 2 * RET_QK:2 * RET_QK + BW]
    gate = u[:, 2 * RET_QK + BW:]
    hmask = _head_lane_masks(hsum_ref[...])
    dec, xi, zeta = dec_ref[...], xi_ref[...], zeta_ref[...]
    kmask, smask, gam = kmask_ref[...], smask_ref[...], gam_ref[...]
    parts = []
    for j in range(tm // CHUNK):
        yield
        sl = slice(j * CHUNK, (j + 1) * CHUNK)
        qj, kj, vj = q[sl], k[sl], v[sl]
        k_blk = jnp.concatenate([kj.astype(BF16)] * N_HEAD, axis=0) * kmask
        sc = _dot_nt(qj, k_blk) * dec
        kv = smask * _dot_tn(kj * zeta, vj)
        parts.append((jnp.concatenate([sc, qj * xi], axis=1), _block(vj, hmask), kv))
    yield
    s = s_ref[...]
    ys = []
    for lhs, v_blk, kv in parts:
        ys.append(_dot(lhs, jnp.concatenate([v_blk, s.astype(BF16)], axis=0)))
        s = gam * s + kv
    s_ref[...] = s
    y = jnp.concatenate(ys, axis=0)
    ms = _head_sum(y * y, hsum_ref[...]) * (1.0 / HEAD)
    return y * lax.rsqrt(ms + EPS) * g_ref[...] * _silu(gate)


def _lru_seq(u, p, tail_ref, h_ref):
    tm = u.shape[0]
    x = u[:, :BW]
    gate = u[:, BW:]
    tail = tail_ref[...]
    cw = p["conv_w"][...]
    y = p["conv_b"][...] + cw[CONV_WIDTH - 1:CONV_WIDTH, :] * x
    for k in range(1, CONV_WIDTH):
        y = y + cw[CONV_WIDTH - 1 - k:CONV_WIDTH - k, :] * _shift_rows(x, k, tail)
    tail_ref[...] = x[tm - 8:, :]
    yield

    r = _sigmoid(_dot(y, p["wa"][...]) + p["ba"][...])
    i = _sigmoid(_dot(y, p["wx"][...]) + p["bx"][...])
    log_a = -LRU_C * r * _softplus(-p["lam"][...])
    a = jnp.exp(log_a)
    mult = jnp.sqrt(jnp.tanh(-log_a) * (a * a + 1.0))
    row = lax.broadcasted_iota(jnp.int32, x.shape, 0)
    mult = jnp.where(row + pl.program_id(1) * tm == 0, 1.0, mult)
    b = mult * (i * y)

    d = 1
    while d < tm:
        yield
        if d % 8:
            a_sh = jnp.where(row < d, 1.0, pltpu.roll(a, d, axis=0))
            b_sh = jnp.where(row < d, 0.0, pltpu.roll(b, d, axis=0))
            b = a * b_sh + b
            a = a * a_sh
        else:
            b = jnp.concatenate([b[:d], a[d:] * b[:tm - d] + b[d:]], axis=0)
            a = jnp.concatenate([a[:d], a[d:] * a[:tm - d]], axis=0)
        d *= 2
    h = a * h_ref[0:1, :] + b
    h_ref[0:1, :] = h[tm - 1:tm, :]
    return h * _silu(gate)


_RWKV_KEYS = ("mu", "w0", "w2", "a0", "a2", "kk", "ka", "rk", "ln_g", "ln_b")
_LRU_KEYS = ("conv_w", "conv_b", "wa", "ba", "wx", "bx", "lam")
_RET_KEYS = ("cos", "sin", "decay", "zeta", "xi", "gam", "kmask", "smask")
_MATRIX_KEYS = ("w2", "a2", "conv_w", "wa", "wx")
N_SCRATCH = 6


def _then_project(mixer, wb_ref, branch):
    y = yield from mixer
    yield
    return _dot(y, wb_ref[branch]).astype(BF16)


def _when_ready(inputs, key, make, lag):
    while key not in inputs:
        yield
    for _ in range(lag):
        yield
    return (yield from make(inputs[key]))


def _feed_all(hns, w_in_ref, d, inputs):
    tm = hns[0].shape[0]
    hn = jnp.concatenate(hns, axis=0)
    per_seq = lambda a: [a[i * tm:(i + 1) * tm] for i in range(len(hns))]
    o = 0
    for key, n in enumerate((RWKV_W, HGRN_W, RET_W, LRU_W)):
        pieces = []
        for c0 in range(0, n, FEED_PIECE):
            w = min(FEED_PIECE, n - c0)
            pieces.append(jnp.dot(hn, w_in_ref[:, o + c0:o + c0 + w], preferred_element_type=F32))
            yield
        for i, u in enumerate(per_seq(jnp.concatenate(pieces, axis=1))):
            inputs[i][key] = u
        o += n
    gates = [[] for _ in hns]
    for bi in range(N_BRANCH):
        pieces = []
        for c0 in range(0, d, FEED_PIECE):
            col = MIX_W + bi * d + c0
            pieces.append(_sigmoid(jnp.dot(hn, w_in_ref[:, col:col + FEED_PIECE],
                                           preferred_element_type=F32).astype(BF16)))
            yield
        for i, g in enumerate(per_seq(jnp.concatenate(pieces, axis=1))):
            gates[i].append(g)
    return gates


def _layer_kernel(*refs, layer, final_norm):
    it = iter(refs)
    take = lambda n: [next(it) for _ in range(n)]
    row = lambda ref: ref.at[layer:layer + 1]
    x_ref, mod_ref, g_ref, w_in_ref = take(4)
    rwkv_p = dict(zip(_RWKV_KEYS, take(len(_RWKV_KEYS))))
    tril_ref, hsum_ref, lb_ref, hg_ref = take(4)
    ret_t = take(len(_RET_KEYS))
    (rg_ref,) = take(1)
    lru_p = dict(zip(_LRU_KEYS, take(len(_LRU_KEYS))))
    wb_ref, wo_ref, fg_ref, o_ref = take(4)
    scratch = take(N_SCRATCH)
    g_ref, lb_ref, hg_ref, rg_ref = row(g_ref), row(lb_ref), row(hg_ref), row(rg_ref)
    rwkv_p = {k: v if k in _MATRIX_KEYS else row(v) for k, v in rwkv_p.items()}
    lru_p = {k: v if k in _MATRIX_KEYS else row(v) for k, v in lru_p.items()}

    @pl.when(pl.program_id(1) == 0)
    def _():
        for ref in scratch:
            ref[...] = jnp.zeros_like(ref)

    ns, _, d = x_ref.shape
    xs = [x_ref[i] for i in range(ns)]
    mods = [mod_ref[i] for i in range(ns)]
    hns = [_adaln(xs[i], g_ref[...], mods[i]).astype(BF16) for i in range(ns)]

    def mixers(i):
        rw_tail, rw_s, hg_s, rt_s, lr_tail, lr_h = (ref.at[i] for ref in scratch)
        return (lambda u: _rwkv_seq(u, rwkv_p, tril_ref, hsum_ref, rw_tail, rw_s),
                lambda u: _hgrn_seq(u, lb_ref, hg_ref, tril_ref, hsum_ref, hg_s),
                lambda u: _ret_seq(u, *ret_t, rg_ref, hsum_ref, rt_s),
                lambda u: _lru_seq(u, lru_p, lr_tail, lr_h))

    inputs = [{} for _ in range(ns)]
    gates, *projs = _run(_round_robin(
        [_feed_all(hns, w_in_ref, d, inputs)]
        + [_then_project(_when_ready(inputs[i], bi, mixers(i)[bi], i * SEQ_LAG), wb_ref, bi)
           for bi in range(N_BRANCH) for i in range(ns)]))
    merged = []
    for i in range(ns):
        terms = [gates[i][bi] * projs[bi * ns + i] for bi in range(N_BRANCH)]
        merged.append((terms[0] + terms[1]) + (terms[2] + terms[3]))
    update = _dot(jnp.concatenate(merged, axis=0), wo_ref[...])
    tm = xs[0].shape[0]
    for i in range(ns):
        out = xs[i] + mods[i][2:3, :] * update[i * tm:(i + 1) * tm]
        if final_norm:
            out = out * lax.rsqrt(jnp.mean(out * out, axis=-1, keepdims=True) + EPS) * fg_ref[...]
        o_ref[i] = out


def _resident(shape, index_map):
    return pl.BlockSpec(shape, index_map, pipeline_mode=pl.Buffered(1))


def _layer(layer, h, mod, p, consts, final_norm):
    nb, nt, d = h.shape
    tm = min(TOKEN_TILE, nt)
    ns = SEQ_PER_STEP if nb % SEQ_PER_STEP == 0 else 1
    const = lambda a: _resident(a.shape, lambda b, t, nd=a.ndim: (0,) * nd)
    per_layer = lambda a: const(a) if a.ndim == 2 else _resident(
        (None,) + a.shape[1:], lambda b, t, nd=a.ndim: (layer,) + (0,) * (nd - 1))
    tok = pl.BlockSpec((ns, tm, d), lambda b, t: (b, t, 0))
    pos = pl.BlockSpec((tm, RET_QK), lambda b, t: (t, 0))
    ret = consts["ret"]
    operands = ([h, mod, p["norm_g"], p["w_in"]] + [p["rwkv"][k] for k in _RWKV_KEYS]
                + [consts["tril"], consts["hsum"], p["hgrn_lb"], p["hgrn_g"]]
                + [ret[k] for k in _RET_KEYS] + [p["ret_g"]] + [p["lru"][k] for k in _LRU_KEYS]
                + [p["w_branch"], p["w_out"], p["final_g"]])
    in_specs = ([tok, pl.BlockSpec((None, ns, 3, d), lambda b, t: (layer, b, 0, 0)),
                 per_layer(p["norm_g"]), per_layer(p["w_in"])]
                + [per_layer(p["rwkv"][k]) for k in _RWKV_KEYS]
                + [const(consts["tril"]), const(consts["hsum"]), per_layer(p["hgrn_lb"]),
                   per_layer(p["hgrn_g"]), pos, pos]
                + [const(ret[k]) for k in _RET_KEYS[2:]] + [per_layer(p["ret_g"])]
                + [per_layer(p["lru"][k]) for k in _LRU_KEYS]
                + [per_layer(p["w_branch"]), per_layer(p["w_out"]), const(p["final_g"])])
    scratch = [pltpu.VMEM((ns, 8, RWKV_SHIFT), F32), pltpu.VMEM((ns, HEAD, BW), F32),
               pltpu.VMEM((ns, HEAD, BW), F32), pltpu.VMEM((ns, RET_QK, BW), F32),
               pltpu.VMEM((ns, 8, BW), F32), pltpu.VMEM((ns, 8, BW), F32)]
    assert len(scratch) == N_SCRATCH
    return pl.pallas_call(
        functools.partial(_layer_kernel, layer=layer, final_norm=final_norm),
        grid=(nb // ns, nt // tm),
        in_specs=in_specs,
        out_specs=tok,
        out_shape=jax.ShapeDtypeStruct((nb, nt, d), F32),
        scratch_shapes=scratch,
        compiler_params=pltpu.CompilerParams(dimension_semantics=("parallel", "arbitrary"),
                                             vmem_limit_bytes=VMEM_LIMIT),
        name="layer",
    )(*operands)


def _block_diag(w):
    depth, g, n, _ = w.shape
    eye = jnp.eye(g, dtype=w.dtype)
    return (w[:, :, :, None, :] * eye[None, :, None, :, None]).reshape(depth, g * n, g * n)


def kernel(x, c, norm_g, w_mod, b_mod, w_in, rwkv_mu, rwkv_w0, rwkv_w2, rwkv_a0, rwkv_a2, rwkv_kk, rwkv_ka, rwkv_rk, rwkv_ln_g, rwkv_ln_b, hgrn_lb, hgrn_norm_g, ret_norm_g, lru_conv_w, lru_conv_b, lru_wa, lru_ba, lru_wx, lru_bx, lru_lam, w_branch, w_out, final_g):
    nb, nt, d = x.shape
    depth = w_in.shape[0]
    tm = min(TOKEN_TILE, nt)
    assert nt % tm == 0 and tm % (2 * CHUNK) == 0 and CHUNK == HEAD
    assert w_in.shape[2] == MIX_W + N_BRANCH * d

    lb_p = jax.nn.softmax(hgrn_lb.astype(F32), axis=0)
    params = dict(
        norm_g=norm_g, w_in=w_in.astype(BF16),
        rwkv=dict(mu=rwkv_mu, w0=rwkv_w0, w2=rwkv_w2.astype(BF16), a0=rwkv_a0,
                  a2=rwkv_a2.astype(BF16), kk=rwkv_kk, ka=rwkv_ka, rk=rwkv_rk,
                  ln_g=rwkv_ln_g, ln_b=rwkv_ln_b),
        hgrn_lb=jnp.cumsum(lb_p, axis=0) - lb_p[0], hgrn_g=hgrn_norm_g, ret_g=ret_norm_g,
        lru=dict(conv_w=lru_conv_w, conv_b=lru_conv_b, wa=_block_diag(lru_wa).astype(BF16),
                 ba=lru_ba, wx=_block_diag(lru_wx).astype(BF16), bx=lru_bx, lam=lru_lam),
        w_branch=w_branch.astype(BF16), w_out=w_out.astype(BF16), final_g=final_g.reshape(1, d))

    head_id = np.arange(BW) // HEAD
    tok = np.arange(tm)
    consts = dict(
        hsum=jnp.asarray((head_id[:, None] == head_id[None, :]).astype(np.float32), dtype=BF16),
        tril=jnp.asarray(((tok[:, None] // CHUNK == tok[None, :] // CHUNK)
                          & (tok[None, :] <= tok[:, None])).astype(np.float32), dtype=BF16),
        ret=dict(zip(_RET_KEYS, _ret_tables(nt))))

    mod = _modulation(c, w_mod, b_mod).reshape(depth, nb, 3, d)
    h = x
    for l in range(depth):
        h = _layer(l, h, mod, params, consts, final_norm=(l == depth - 1))
    return h
```

```python
import functools

import numpy as np
import jax
import jax.numpy as jnp
from jax import lax
from jax.experimental import pallas as pl
from jax.experimental.pallas import tpu as pltpu

F32 = jnp.float32
BF16 = jnp.bfloat16

N_BRANCH = 4
BW = 256
N_HEAD = 4
HEAD = BW // N_HEAD
LANE = 128
EPS = 1e-6
RWKV_LORA = 64
RWKV_SHIFT = 3 * BW + 2 * RWKV_LORA
RWKV_W = RWKV_SHIFT + BW
RWKV_LN_EPS = 64e-5
HGRN_W = 4 * BW
RET_QKHEAD = HEAD // 2
RET_QK = N_HEAD * RET_QKHEAD
RET_W = 2 * RET_QK + 2 * BW
ROPE_BASE = 10000.0
LRU_W = 2 * BW
CONV_WIDTH = 4
LRU_C = 8.0
MIX_W = RWKV_W + HGRN_W + RET_W + LRU_W

CHUNK = 64
N_LEVEL = 6
TOKEN_TILE = 128
SEQ_PER_STEP = 4
FEED_PIECE = 256
SEQ_LAG = 2
VMEM_LIMIT = 56 * 1024 * 1024


def _dot(a, b):
    return jnp.dot(a.astype(BF16), b.astype(BF16), preferred_element_type=F32)


def _dot_nt(a, b):
    return lax.dot_general(a.astype(BF16), b.astype(BF16), (((1,), (1,)), ((), ())),
                           preferred_element_type=F32)


def _dot_tn(a, b):
    return lax.dot_general(a.astype(BF16), b.astype(BF16), (((0,), (0,)), ((), ())),
                           preferred_element_type=F32)


def _split2(x):
    hi = x.astype(BF16)
    lo = (x - hi.astype(F32)).astype(BF16)
    return hi, lo


def _dot_hi(a, b):
    ah, al = _split2(a)
    bh, bl = _split2(b)
    d = functools.partial(jnp.dot, preferred_element_type=F32)
    return d(ah, bh) + (d(ah, bl) + d(al, bh))


def _dot01_left(m01, x):
    d = functools.partial(jnp.dot, preferred_element_type=F32)
    hi, lo = _split2(x)
    return d(m01, hi) + d(m01, lo)


def _head_sum(x, m01):
    return jnp.dot(x.astype(BF16), m01, preferred_element_type=F32)


def _sigmoid(x):
    return 0.5 * jnp.tanh(0.5 * x) + 0.5


def _silu(x):
    return x * _sigmoid(x)


def _softplus(x):
    return jnp.maximum(x, 0.0) + jnp.log1p(jnp.exp(-jnp.abs(x)))


def _adaln(x, g, mod):
    ms = jnp.mean(x * x, axis=-1, keepdims=True)
    y = x * lax.rsqrt(ms + EPS) * g
    return y * (1.0 + mod[1:2, :]) + mod[0:1, :]


def _shift_rows(x, k, tail):
    xr = pltpu.roll(x, k, axis=0)
    row = lax.broadcasted_iota(jnp.int32, (8, x.shape[1]), 0)
    head = jnp.where(row < k, pltpu.roll(tail, k, axis=0), xr[0:8, :])
    return jnp.concatenate([head, xr[8:, :]], axis=0)


def _round_robin(gens):
    gens = list(gens)
    results = [None] * len(gens)
    alive = list(range(len(gens)))
    while alive:
        still = []
        for idx in alive:
            try:
                next(gens[idx])
                still.append(idx)
            except StopIteration as stop:
                results[idx] = stop.value
        alive = still
        yield
    return results


def _run(gen):
    try:
        while True:
            next(gen)
    except StopIteration as stop:
        return stop.value


def _mod_kernel(c_ref, w_ref, b_ref, o_ref):
    o_ref[...] = _dot_hi(_silu(c_ref[...]), w_ref[...]) + b_ref[...]


def _modulation(c, w_mod, b_mod):
    depth, d, d3 = w_mod.shape
    nb = c.shape[0]
    n_col = d3 // d
    return pl.pallas_call(
        _mod_kernel,
        grid=(depth, n_col),
        in_specs=[pl.BlockSpec((nb, d), lambda l, j: (0, 0)),
                  pl.BlockSpec((None, d, d), lambda l, j: (l, 0, j)),
                  pl.BlockSpec((None, 1, d), lambda l, j: (l, 0, j))],
        out_specs=pl.BlockSpec((None, nb, d), lambda l, j: (l, 0, j)),
        out_shape=jax.ShapeDtypeStruct((depth, nb, d3), F32),
        name="modulation",
    )(c, w_mod, b_mod.reshape(depth, 1, d3))


def _head_lane_masks(bd01):
    return [bd01[i * HEAD:i * HEAD + CHUNK, 0:LANE] for i in range(LANE // HEAD)]


def _block(x, masks):
    xb = x.astype(BF16)
    zero = jnp.zeros((x.shape[0], LANE), BF16)
    per = LANE // HEAD
    rows = []
    for h in range(N_HEAD):
        g = h // per
        part = xb[:, g * LANE:(g + 1) * LANE] * masks[h % per]
        rows.append(jnp.concatenate([part if i == g else zero for i in range(BW // LANE)], axis=1))
    return jnp.concatenate(rows, axis=0)


def _rows_to_wide(xt, c0):
    return jnp.concatenate([xt[h * HEAD:(h + 1) * HEAD, c0:c0 + CHUNK] for h in range(N_HEAD)],
                           axis=1)


def _unit_lower_inverses(a_list, eye_wide, hmask):
    n = len(a_list)
    xs = [_dot(a, _block(a, hmask)) for a in a_list]
    invs = [eye_wide + a for a in a_list]
    for lvl in range(1, N_LEVEL):
        yield
        for i in range(n):
            xb = _block(xs[i], hmask)
            if lvl < N_LEVEL - 1:
                res = _dot(jnp.concatenate([xs[i], invs[i]], axis=0), xb)
                xs[i] = res[:CHUNK]
                invs[i] = invs[i] + res[CHUNK:]
            else:
                invs[i] = invs[i] + _dot(invs[i], xb)
    return invs


def _rwkv_seq(u, p, tril_ref, hsum_ref, tail_ref, s_ref):
    tm = u.shape[0]
    feats = u[:, :RWKV_SHIFT]
    gate = u[:, RWKV_SHIFT:]
    prev = _shift_rows(feats, 1, tail_ref[...])
    tail_ref[...] = feats[tm - 8:, :]
    x = feats + p["mu"][...] * (prev - feats)
    r = x[:, 0:BW]
    k = x[:, BW:2 * BW]
    v = x[:, 2 * BW:3 * BW]
    w_lo = x[:, 3 * BW:3 * BW + RWKV_LORA]
    a_lo = x[:, 3 * BW + RWKV_LORA:]

    hsum = hsum_ref[...]
    log_w = -float(np.exp(-0.5)) * _sigmoid(p["w0"][...] + _dot(jnp.tanh(w_lo), p["w2"][...]))
    a = _sigmoid(p["a0"][...] + _dot(a_lo, p["a2"][...]))
    kk = k * p["kk"][...]
    k = k * (1.0 + (a - 1.0) * p["ka"][...])
    kk = kk / jnp.maximum(jnp.sqrt(_head_sum(kk * kk, hsum)), 1e-12)
    beta = kk * a

    c = _dot01_left(tril_ref[...], log_w)
    n_chunk = tm // CHUNK
    hmask = _head_lane_masks(hsum)
    row = lax.broadcasted_iota(jnp.int32, (CHUNK, BW), 0)
    pos = lax.broadcasted_iota(jnp.int32, (CHUNK, BW), 1) & (HEAD - 1)
    strict = row > pos
    lower = row >= pos
    eye_wide = (row == pos).astype(F32)

    al_t, r_t, a_ab, a_ak, a_rb, a_rk, b_eT, kv_c, e_col, v_blk = ([] for _ in range(10))
    yield
    for j in range(n_chunk):
        sl = slice(j * CHUNK, (j + 1) * CHUNK)
        cj = c[sl]
        c_last = cj[CHUNK - 1:CHUNK, :]
        e_in = jnp.exp(cj)
        e_prev = jnp.exp(cj - log_w[sl])
        e_out = jnp.exp(-cj)
        e_rest = jnp.exp(c_last) * e_out
        al_t.append(-kk[sl] * e_prev)
        r_t.append(r[sl] * e_in)
        rhs = jnp.concatenate([_block(beta[sl] * e_out, hmask), _block(k[sl] * e_out, hmask)], axis=0)
        pm = _dot_nt(jnp.concatenate([al_t[j], r_t[j]], axis=0), rhs)
        a_ab.append(jnp.where(strict, pm[:CHUNK, :BW], 0.0))
        a_ak.append(jnp.where(strict, pm[:CHUNK, BW:], 0.0))
        a_rb.append(jnp.where(lower, pm[CHUNK:, :BW], 0.0))
        a_rk.append(jnp.where(lower, pm[CHUNK:, BW:], 0.0))
        ends = jnp.concatenate([beta[sl] * e_rest, k[sl] * e_rest], axis=0).T
        b_eT.append(_rows_to_wide(ends, 0))
        v_blk.append(_block(v[sl], hmask))
        kv_c.append(_dot(_rows_to_wide(ends, CHUNK), v_blk[j]))
        col = jnp.exp(jnp.broadcast_to(c_last, (2 * CHUNK, BW)).T)
        e_col.append(_rows_to_wide(col, 0))
        yield
    avs = [_dot(jnp.concatenate([a_ak[j], a_rk[j]], axis=0), v_blk[j]) for j in range(n_chunk)]
    invs = yield from _unit_lower_inverses(a_ab, eye_wide, hmask)
    yield
    tws = [_dot(invs[j], jnp.concatenate([_block(al_t[j], hmask), _block(avs[j][:CHUNK], hmask)],
                                         axis=1)) for j in range(n_chunk)]
    t_al = [tw[:, :BW] for tw in tws]
    t_akv = [tw[:, BW:] for tw in tws]
    yield
    qzs = [_dot(a_rb[j], jnp.concatenate([_block(t_al[j], hmask), _block(t_akv[j], hmask)], axis=1))
           for j in range(n_chunk)]
    q_c = [r_t[j] + qzs[j][:, :BW] for j in range(n_chunk)]
    z_c = [qzs[j][:, BW:] + avs[j][CHUNK:] for j in range(n_chunk)]
    yield

    s = s_ref[...]
    ys = []
    for j in range(n_chunk):
        uo = _dot(jnp.concatenate([t_al[j], q_c[j]], axis=0), _block(s, hmask))
        ys.append(uo[CHUNK:] + z_c[j])
        uu = uo[:CHUNK] + t_akv[j]
        yield
        s = e_col[j] * s + _dot(b_eT[j], _block(uu, hmask)) + kv_c[j]
        yield
    s_ref[...] = s
    y = jnp.concatenate(ys, axis=0)

    mean = _head_sum(y, hsum) * (1.0 / HEAD)
    yc = y - mean
    var = _head_sum(yc * yc, hsum) * (1.0 / HEAD)
    y = yc * lax.rsqrt(var + RWKV_LN_EPS) * p["ln_g"][...] + p["ln_b"][...]
    bonus = _head_sum(r * k * p["rk"][...], hsum) * v
    return (y + bonus) * _silu(gate)


def _midpoint_rows(b, lvl, row):
    blk = CHUNK >> lvl
    half = blk // 2
    if blk >= 8:
        return jnp.concatenate(
            [jnp.broadcast_to(b[s + half - 1:s + half, :], (blk, b.shape[1]))
             for s in range(0, CHUNK, blk)], axis=0)
    p = row & (blk - 1)
    out = b
    for d in range(-(half - 1), half + 1):
        if d != 0:
            out = jnp.where(p == half - 1 + d, pltpu.roll(b, d % CHUNK, axis=0), out)
    return out


def _hgrn_seq(u, lb_ref, g_ref, tril_ref, hsum_ref, s_ref):
    tm = u.shape[0]
    q = u[:, 0:BW]
    kx = (1.0 - lb_ref[...]) * _sigmoid(-u[:, BW:2 * BW])
    log_f = jnp.log(1.0 - kx)
    v = u[:, 2 * BW:3 * BW]
    gate = u[:, 3 * BW:]
    b_all = _dot01_left(tril_ref[...], log_f)

    hmask = _head_lane_masks(hsum_ref[...])
    row = lax.broadcasted_iota(jnp.int32, (CHUNK, BW), 0)
    pos = lax.broadcasted_iota(jnp.int32, (CHUNK, BW), 1) & (HEAD - 1)
    masks = []
    for lvl in range(N_LEVEL):
        sh = N_LEVEL - lvl
        half = 1 << (sh - 1)
        same = (row >> sh) == (pos >> sh)
        masks.append(same & ((row & half) != 0) & ((pos & half) == 0))
    diag = row == pos

    yield
    parts = []
    for j in range(0, tm // CHUNK, 2):
        pair = []
        for jj in (j, j + 1):
            sl = slice(jj * CHUNK, (jj + 1) * CHUNK)
            qj, kj, vj, b = q[sl], kx[sl], v[sl], b_all[sl]
            b_last = b[CHUNK - 1:CHUNK, :]
            qb, kb = qj.astype(BF16), kj.astype(BF16)
            sc = jnp.where(diag, _dot_nt(qb, _block(kb, hmask)), 0.0)
            for lvl in range(N_LEVEL):
                e = jnp.exp(-jnp.abs(b - _midpoint_rows(b, lvl, row))).astype(BF16)
                sc = jnp.where(masks[lvl], _dot_nt(qb * e, _block(kb * e, hmask)), sc)
            col = jnp.exp(jnp.broadcast_to(b_last, (2 * CHUNK, BW)).T)
            pair.append((sc, qj * jnp.exp(b), _block(vj, hmask), kj * jnp.exp(b_last - b),
                         _rows_to_wide(col, 0)))
            yield
        k_eT = jnp.concatenate([pair[0][3], pair[1][3]], axis=0).T
        for i, (sc, q_in, v_blk, _, e_col) in enumerate(pair):
            kv = _dot(_rows_to_wide(k_eT, i * CHUNK), v_blk)
            parts.append((jnp.concatenate([sc, q_in], axis=1), v_blk, e_col, kv))
        yield
    s = s_ref[...]
    ys = []
    for lhs, v_blk, e_col, kv in parts:
        ys.append(_dot(lhs, jnp.concatenate([v_blk, _block(s, hmask)], axis=0)))
        s = e_col * s + kv
        yield
    s_ref[...] = s
    y = jnp.concatenate(ys, axis=0)
    ms = _head_sum(y * y, hsum_ref[...]) * (1.0 / HEAD)
    return y * lax.rsqrt(ms + EPS) * g_ref[...] * _silu(gate)


def _ret_tables(nt):
    dk, c = RET_QKHEAD, CHUNK
    pos = jnp.arange(nt, dtype=F32)
    inv_freq = 1.0 / (ROPE_BASE ** jnp.linspace(0.0, 1.0, dk // 2, dtype=F32))
    ang = pos[:, None] * inv_freq[None, :]
    cos, sin = jnp.cos(ang), jnp.sin(ang)
    cos_t = jnp.tile(jnp.concatenate([cos, cos], -1), (1, N_HEAD))
    sin_t = jnp.tile(jnp.concatenate([-sin, sin], -1), (1, N_HEAD))
    log_gamma = jnp.log1p(-jnp.exp2(-5.0 - jnp.arange(N_HEAD, dtype=F32)))
    idx = jnp.arange(c, dtype=F32)
    rel = idx[:, None] - idx[None, :]
    decay = jnp.where(rel >= 0, jnp.exp(log_gamma[:, None, None] * jnp.maximum(rel, 0.0)), 0.0)
    decay_wide = jnp.concatenate([decay[h] for h in range(N_HEAD)], axis=1)
    zeta = jnp.exp(log_gamma[:, None] * (c - 1 - idx))
    xi = jnp.exp(log_gamma[:, None] * (idx + 1.0))
    gamma_c = jnp.exp(log_gamma * c)
    lanes = lambda t: jnp.repeat(t.T, dk, axis=1)
    gam = jnp.broadcast_to(jnp.repeat(gamma_c, dk)[:, None], (RET_QK, BW))
    key_head = np.arange(RET_QK) // dk
    kmask = jnp.asarray((np.repeat(np.arange(N_HEAD), c)[:, None] == key_head[None, :])
                        .astype(np.float32), dtype=BF16)
    smask = jnp.asarray((key_head[:, None] == (np.arange(BW) // HEAD)[None, :]).astype(np.float32))
    return cos_t, sin_t, decay_wide, lanes(zeta), lanes(xi), gam, kmask, smask


def _ret_seq(u, cos_ref, sin_ref, dec_ref, zeta_ref, xi_ref, gam_ref, kmask_ref, smask_ref, g_ref,
             hsum_ref, s_ref):
    tm = u.shape[0]
    cos, sin = cos_ref[...], sin_ref[...]
    lane = lax.broadcasted_iota(jnp.int32, cos.shape, 1)
    first_half = (lane & (RET_QKHEAD // 2)) == 0

    def rope(t):
        half = RET_QKHEAD // 2
        swapped = jnp.where(first_half, pltpu.roll(t, RET_QK - half, axis=1),
                            pltpu.roll(t, half, axis=1))
        return t * cos + swapped * sin

    q = rope(u[:, 0:RET_QK])
    k = rope(u[:, RET_QK:2 * RET_QK]) * (RET_QKHEAD ** -0.5)
    v = u[:, 2 * RET_QK:2 * RET_QK + BW]
    gate = u[:, 2 * RET_QK + BW:]
    hmask = _head_lane_masks(hsum_ref[...])
    dec, xi, zeta = dec_ref[...], xi_ref[...], zeta_ref[...]
    kmask, smask, gam = kmask_ref[...], smask_ref[...], gam_ref[...]
    parts = []
    for j in range(tm // CHUNK):
        yield
        sl = slice(j * CHUNK, (j + 1) * CHUNK)
        qj, kj, vj = q[sl], k[sl], v[sl]
        k_blk = jnp.concatenate([kj.astype(BF16)] * N_HEAD, axis=0) * kmask
        sc = _dot_nt(qj, k_blk) * dec
        kv = smask * _dot_tn(kj * zeta, vj)
        parts.append((jnp.concatenate([sc, qj * xi], axis=1), _block(vj, hmask), kv))
    yield
    s = s_ref[...]
    ys = []
    for lhs, v_blk, kv in parts:
        ys.append(_dot(lhs, jnp.concatenate([v_blk, s.astype(BF16)], axis=0)))
        s = gam * s + kv
    s_ref[...] = s
    y = jnp.concatenate(ys, axis=0)
    ms = _head_sum(y * y, hsum_ref[...]) * (1.0 / HEAD)
    return y * lax.rsqrt(ms + EPS) * g_ref[...] * _silu(gate)


def _lru_seq(u, p, tail_ref, h_ref):
    tm = u.shape[0]
    x = u[:, :BW]
    gate = u[:, BW:]
    tail = tail_ref[...]
    cw = p["conv_w"][...]
    y = p["conv_b"][...] + cw[CONV_WIDTH - 1:CONV_WIDTH, :] * x
    for k in range(1, CONV_WIDTH):
        y = y + cw[CONV_WIDTH - 1 - k:CONV_WIDTH - k, :] * _shift_rows(x, k, tail)
    tail_ref[...] = x[tm - 8:, :]
    yield

    r = _sigmoid(_dot(y, p["wa"][...]) + p["ba"][...])
    i = _sigmoid(_dot(y, p["wx"][...]) + p["bx"][...])
    log_a = -LRU_C * r * _softplus(-p["lam"][...])
    a = jnp.exp(log_a)
    mult = jnp.sqrt(jnp.tanh(-log_a) * (a * a + 1.0))
    row = lax.broadcasted_iota(jnp.int32, x.shape, 0)
    mult = jnp.where(row + pl.program_id(1) * tm == 0, 1.0, mult)
    b = mult * (i * y)

    d = 1
    while d < tm:
        yield
        if d % 8:
            a_sh = jnp.where(row < d, 1.0, pltpu.roll(a, d, axis=0))
            b_sh = jnp.where(row < d, 0.0, pltpu.roll(b, d, axis=0))
            b = a * b_sh + b
            a = a * a_sh
        else:
            b = jnp.concatenate([b[:d], a[d:] * b[:tm - d] + b[d:]], axis=0)
            a = jnp.concatenate([a[:d], a[d:] * a[:tm - d]], axis=0)
        d *= 2
    h = a * h_ref[0:1, :] + b
    h_ref[0:1, :] = h[tm - 1:tm, :]
    return h * _silu(gate)


_RWKV_KEYS = ("mu", "w0", "w2", "a0", "a2", "kk", "ka", "rk", "ln_g", "ln_b")
_LRU_KEYS = ("conv_w", "conv_b", "wa", "ba", "wx", "bx", "lam")
_RET_KEYS = ("cos", "sin", "decay", "zeta", "xi", "gam", "kmask", "smask")
_MATRIX_KEYS = ("w2", "a2", "conv_w", "wa", "wx")
N_SCRATCH = 6


def _then_project(mixer, wb_ref, branch):
    y = yield from mixer
    yield
    return _dot(y, wb_ref[branch]).astype(BF16)


def _when_ready(inputs, key, make, lag):
    while key not in inputs:
        yield
    for _ in range(lag):
        yield
    return (yield from make(inputs[key]))


def _feed_all(hns, w_in_ref, d, inputs):
    tm = hns[0].shape[0]
    hn = jnp.concatenate(hns, axis=0)
    per_seq = lambda a: [a[i * tm:(i + 1) * tm] for i in range(len(hns))]
    o = 0
    for key, n in enumerate((RWKV_W, HGRN_W, RET_W, LRU_W)):
        pieces = []
        for c0 in range(0, n, FEED_PIECE):
            w = min(FEED_PIECE, n - c0)
            pieces.append(jnp.dot(hn, w_in_ref[:, o + c0:o + c0 + w], preferred_element_type=F32))
            yield
        for i, u in enumerate(per_seq(jnp.concatenate(pieces, axis=1))):
            inputs[i][key] = u
        o += n
    gates = [[] for _ in hns]
    for bi in range(N_BRANCH):
        pieces = []
        for c0 in range(0, d, FEED_PIECE):
            col = MIX_W + bi * d + c0
            pieces.append(_sigmoid(jnp.dot(hn, w_in_ref[:, col:col + FEED_PIECE],
                                           preferred_element_type=F32).astype(BF16)))
            yield
        for i, g in enumerate(per_seq(jnp.concatenate(pieces, axis=1))):
            gates[i].append(g)
    return gates


def _layer_kernel(*refs, layer, final_norm):
    it = iter(refs)
    take = lambda n: [next(it) for _ in range(n)]
    row = lambda ref: ref.at[layer:layer + 1]
    x_ref, mod_ref, g_ref, w_in_ref = take(4)
    rwkv_p = dict(zip(_RWKV_KEYS, take(len(_RWKV_KEYS))))
    tril_ref, hsum_ref, lb_ref, hg_ref = take(4)
    ret_t = take(len(_RET_KEYS))
    (rg_ref,) = take(1)
    lru_p = dict(zip(_LRU_KEYS, take(len(_LRU_KEYS))))
    wb_ref, wo_ref, fg_ref, o_ref = take(4)
    scratch = take(N_SCRATCH)
    g_ref, lb_ref, hg_ref, rg_ref = row(g_ref), row(lb_ref), row(hg_ref), row(rg_ref)
    rwkv_p = {k: v if k in _MATRIX_KEYS else row(v) for k, v in rwkv_p.items()}
    lru_p = {k: v if k in _MATRIX_KEYS else row(v) for k, v in lru_p.items()}

    @pl.when(pl.program_id(1) == 0)
    def _():
        for ref in scratch:
            ref[...] = jnp.zeros_like(ref)

    ns, _, d = x_ref.shape
    xs = [x_ref[i] for i in range(ns)]
    mods = [mod_ref[i] for i in range(ns)]
    hns = [_adaln(xs[i], g_ref[...], mods[i]).astype(BF16) for i in range(ns)]

    def mixers(i):
        rw_tail, rw_s, hg_s, rt_s, lr_tail, lr_h = (ref.at[i] for ref in scratch)
        return (lambda u: _rwkv_seq(u, rwkv_p, tril_ref, hsum_ref, rw_tail, rw_s),
                lambda u: _hgrn_seq(u, lb_ref, hg_ref, tril_ref, hsum_ref, hg_s),
                lambda u: _ret_seq(u, *ret_t, rg_ref, hsum_ref, rt_s),
                lambda u: _lru_seq(u, lru_p, lr_tail, lr_h))

    inputs = [{} for _ in range(ns)]
    gates, *projs = _run(_round_robin(
        [_feed_all(hns, w_in_ref, d, inputs)]
        + [_then_project(_when_ready(inputs[i], bi, mixers(i)[bi], i * SEQ_LAG), wb_ref, bi)
           for bi in range(N_BRANCH) for i in range(ns)]))
    merged = []
    for i in range(ns):
        terms = [gates[i][bi] * projs[bi * ns + i] for bi in range(N_BRANCH)]
        merged.append((terms[0] + terms[1]) + (terms[2] + terms[3]))
    merged = jnp.concatenate(merged, axis=0)
    tm = xs[0].shape[0]
    sq = [None] * ns
    for c0 in range(0, d, FEED_PIECE):
        cols = slice(c0, c0 + FEED_PIECE)
        update = _dot(merged, wo_ref[:, cols])
        for i in range(ns):
            out = x_ref[i, :, cols] + mod_ref[i, 2:3, cols] * update[i * tm:(i + 1) * tm]
            o_ref[i, :, cols] = out
            if final_norm:
                part = jnp.sum(out * out, axis=-1, keepdims=True)
                sq[i] = part if sq[i] is None else sq[i] + part
    if final_norm:
        for i in range(ns):
            o_ref[i] = o_ref[i] * lax.rsqrt(sq[i] * (1.0 / d) + EPS) * fg_ref[...]


def _resident(shape, index_map):
    return pl.BlockSpec(shape, index_map, pipeline_mode=pl.Buffered(1))


def _layer(layer, h, mod, p, consts, final_norm):
    nb, nt, d = h.shape
    tm = min(TOKEN_TILE, nt)
    ns = SEQ_PER_STEP if nb % SEQ_PER_STEP == 0 else 1
    const = lambda a: _resident(a.shape, lambda b, t, nd=a.ndim: (0,) * nd)
    per_layer = lambda a: const(a) if a.ndim == 2 else _resident(
        (None,) + a.shape[1:], lambda b, t, nd=a.ndim: (layer,) + (0,) * (nd - 1))
    tok = pl.BlockSpec((ns, tm, d), lambda b, t: (b, t, 0))
    pos = pl.BlockSpec((tm, RET_QK), lambda b, t: (t, 0))
    ret = consts["ret"]
    operands = ([h, mod, p["norm_g"], p["w_in"]] + [p["rwkv"][k] for k in _RWKV_KEYS]
                + [consts["tril"], consts["hsum"], p["hgrn_lb"], p["hgrn_g"]]
                + [ret[k] for k in _RET_KEYS] + [p["ret_g"]] + [p["lru"][k] for k in _LRU_KEYS]
                + [p["w_branch"], p["w_out"], p["final_g"]])
    in_specs = ([tok, pl.BlockSpec((None, ns, 3, d), lambda b, t: (layer, b, 0, 0)),
                 per_layer(p["norm_g"]), per_layer(p["w_in"])]
                + [per_layer(p["rwkv"][k]) for k in _RWKV_KEYS]
                + [const(consts["tril"]), const(consts["hsum"]), per_layer(p["hgrn_lb"]),
                   per_layer(p["hgrn_g"]), pos, pos]
                + [const(ret[k]) for k in _RET_KEYS[2:]] + [per_layer(p["ret_g"])]
                + [per_layer(p["lru"][k]) for k in _LRU_KEYS]
                + [per_layer(p["w_branch"]), per_layer(p["w_out"]), const(p["final_g"])])
    scratch = [pltpu.VMEM((ns, 8, RWKV_SHIFT), F32), pltpu.VMEM((ns, HEAD, BW), F32),
               pltpu.VMEM((ns, HEAD, BW), F32), pltpu.VMEM((ns, RET_QK, BW), F32),
               pltpu.VMEM((ns, 8, BW), F32), pltpu.VMEM((ns, 8, BW), F32)]
    assert len(scratch) == N_SCRATCH
    return pl.pallas_call(
        functools.partial(_layer_kernel, layer=layer, final_norm=final_norm),
        grid=(nb // ns, nt // tm),
        in_specs=in_specs,
        out_specs=tok,
        out_shape=jax.ShapeDtypeStruct((nb, nt, d), F32),
        scratch_shapes=scratch,
        compiler_params=pltpu.CompilerParams(dimension_semantics=("parallel", "arbitrary"),
                                             vmem_limit_bytes=VMEM_LIMIT),
        name="layer",
    )(*operands)


def _block_diag(w):
    depth, g, n, _ = w.shape
    eye = jnp.eye(g, dtype=w.dtype)
    return (w[:, :, :, None, :] * eye[None, :, None, :, None]).reshape(depth, g * n, g * n)


def kernel(x, c, norm_g, w_mod, b_mod, w_in, rwkv_mu, rwkv_w0, rwkv_w2, rwkv_a0, rwkv_a2, rwkv_kk, rwkv_ka, rwkv_rk, rwkv_ln_g, rwkv_ln_b, hgrn_lb, hgrn_norm_g, ret_norm_g, lru_conv_w, lru_conv_b, lru_wa, lru_ba, lru_wx, lru_bx, lru_lam, w_branch, w_out, final_g):
    nb, nt, d = x.shape
    depth = w_in.shape[0]
    tm = min(TOKEN_TILE, nt)
    assert nt % tm == 0 and tm % (2 * CHUNK) == 0 and CHUNK == HEAD
    assert w_in.shape[2] == MIX_W + N_BRANCH * d

    lb_p = jax.nn.softmax(hgrn_lb.astype(F32), axis=0)
    params = dict(
        norm_g=norm_g, w_in=w_in.astype(BF16),
        rwkv=dict(mu=rwkv_mu, w0=rwkv_w0, w2=rwkv_w2.astype(BF16), a0=rwkv_a0,
                  a2=rwkv_a2.astype(BF16), kk=rwkv_kk, ka=rwkv_ka, rk=rwkv_rk,
                  ln_g=rwkv_ln_g, ln_b=rwkv_ln_b),
        hgrn_lb=jnp.cumsum(lb_p, axis=0) - lb_p[0], hgrn_g=hgrn_norm_g, ret_g=ret_norm_g,
        lru=dict(conv_w=lru_conv_w, conv_b=lru_conv_b, wa=_block_diag(lru_wa).astype(BF16),
                 ba=lru_ba, wx=_block_diag(lru_wx).astype(BF16), bx=lru_bx, lam=lru_lam),
        w_branch=w_branch.astype(BF16), w_out=w_out.astype(BF16), final_g=final_g.reshape(1, d))

    head_id = np.arange(BW) // HEAD
    tok = np.arange(tm)
    consts = dict(
        hsum=jnp.asarray((head_id[:, None] == head_id[None, :]).astype(np.float32), dtype=BF16),
        tril=jnp.asarray(((tok[:, None] // CHUNK == tok[None, :] // CHUNK)
                          & (tok[None, :] <= tok[:, None])).astype(np.float32), dtype=BF16),
        ret=dict(zip(_RET_KEYS, _ret_tables(nt))))

    mod = _modulation(c, w_mod, b_mod).reshape(depth, nb, 3, d)
    h = x
    for l in range(depth):
        h = _layer(l, h, mod, params, consts, final_norm=(l == depth - 1))
    return h
```
